```python
import math
import jax, jax.numpy as jnp
from jax import lax
import numpy as np

D_MODEL = 1024
BATCH = 8
SEQ = 4096
DEPTH = 1

HG_HEADS = 8
HG_DK = 128
HG_DV = 128
HG_WIDTH = HG_HEADS * HG_DK
HG_CHUNK = 32
POOL_WINDOWS = (2, 4, 8, 16)
POOL_GROUPS = 4
POOL_GROUP = 128
POOL_WIDTH = POOL_GROUPS * POOL_GROUP
POOL_OUT_GROUP = D_MODEL // POOL_GROUPS
MAX_WIN = 16
IN_WIDTHS = (HG_WIDTH, HG_WIDTH, HG_WIDTH, HG_WIDTH, POOL_WIDTH, D_MODEL, D_MODEL)
IN_COLS = sum(IN_WIDTHS)
IN_SPLITS = tuple(int(s) for s in np.cumsum(IN_WIDTHS)[:-1])
PEER_HEADS = 8
PEER_NKEYS = 128
PEER_EXPERTS = PEER_NKEYS * PEER_NKEYS
PEER_QDIM = 256
PEER_HALF = PEER_QDIM // 2
PEER_TOPK = 16
PEER_TOKEN_BLOCK = 128
PLE_DIM = 256
ALPHA = (2.0 * DEPTH) ** 0.25
BETA = (8.0 * DEPTH) ** -0.25
LN_EPS = 1e-5
RMS_EPS = 1e-6

kernel_name = "hybrid_hgrn2_pool_peer_deepnorm_block"


def layer_norm(x, g, b):
    xf = x.astype(jnp.float32)
    mu = jnp.mean(xf, axis=-1, keepdims=True)
    var = jnp.mean(jnp.square(xf - mu), axis=-1, keepdims=True)
    return ((xf - mu) * lax.rsqrt(var + LN_EPS) * g + b).astype(x.dtype)


def hgrn2_mixer(q, f_logit, i, g, lb, norm_g):
    B, S, _ = q.shape
    f32 = jnp.float32
    nc = S // HG_CHUNK
    f = lb + (1.0 - lb) * jax.nn.sigmoid(f_logit.astype(f32))
    k = 1.0 - f
    logf = jnp.log(f)

    def heads(t, d):
        return t.reshape(B, nc, HG_CHUNK, HG_HEADS, d).transpose(0, 3, 1, 2, 4)

    qh = heads(jax.nn.silu(q.astype(f32)) * (HG_DK ** -0.5), HG_DK)
    kh = heads(k, HG_DK)
    vh = heads(i.astype(f32), HG_DV)
    bh = jnp.cumsum(heads(logf, HG_DK), axis=3)
    q_dec = qh * jnp.exp(bh)
    k_inv = kh * jnp.exp(-bh)
    causal = jnp.tril(jnp.ones((HG_CHUNK, HG_CHUNK), dtype=bool))
    scores = jnp.where(causal, jnp.einsum('bhncd,bhnsd->bhncs', q_dec, k_inv), 0.0)
    o_intra = jnp.einsum('bhncs,bhnse->bhnce', scores, vh)
    b_last = bh[:, :, :, -1:, :]
    k_end = kh * jnp.exp(b_last - bh)
    d_state = jnp.einsum('bhncd,bhnce->nbhde', k_end, vh)
    chunk_decay = jnp.exp(b_last[:, :, :, 0, :]).transpose(2, 0, 1, 3)

    def step(state, inp):
        dec, ds = inp
        return dec[..., None] * state + ds, state

    s0 = jnp.zeros((B, HG_HEADS, HG_DK, HG_DV), f32)
    _, s_in = lax.scan(step, s0, (chunk_decay, d_state))
    o_inter = jnp.einsum('bhncd,nbhde->bhnce', q_dec, s_in)
    o = (o_intra + o_inter).transpose(0, 2, 3, 1, 4).reshape(B, S, HG_HEADS, HG_DV)
    o = o * lax.rsqrt(jnp.mean(jnp.square(o), axis=-1, keepdims=True) + RMS_EPS) * norm_g
    o = o.reshape(B, S, HG_HEADS * HG_DV) * jax.nn.silu(g.astype(f32))
    return o.astype(q.dtype)


def pool_mixer(v, w_grp, scale):
    B, S, _ = v.shape
    vf = v.astype(jnp.float32)
    c = jnp.pad(jnp.cumsum(vf, axis=1), ((0, 0), (MAX_WIN, 0), (0, 0)))
    pos = jnp.arange(S)
    outs = []
    for gi, w in enumerate(POOL_WINDOWS):
        lo, hi = gi * POOL_GROUP, (gi + 1) * POOL_GROUP
        wsum = c[:, MAX_WIN:, lo:hi] - c[:, MAX_WIN - w:MAX_WIN - w + S, lo:hi]
        cnt = jnp.minimum(pos + 1, w).astype(jnp.float32)[None, :, None]
        outs.append(wsum / cnt - vf[:, :, lo:hi])
    pooled = jnp.stack(outs, axis=2).astype(v.dtype)
    y = jnp.einsum('bsgc,gco->bsgo', pooled, w_grp).reshape(B, S, D_MODEL)
    return y * scale


def peer_ffn(x, w_query, sub_keys, u_tab, v_tab):
    B, S, D = x.shape
    q = (x @ w_query).reshape(B, S, PEER_HEADS, 2, PEER_HALF)
    sc = jnp.einsum('bshpk,hpnk->bshpn', q, sub_keys).astype(jnp.float32)
    top_v, top_i = lax.top_k(sc, PEER_TOPK)
    cand = top_v[..., 0, :, None] + top_v[..., 1, None, :]
    cand = cand.reshape(B, S, PEER_HEADS, PEER_TOPK * PEER_TOPK)
    best_v, best_pos = lax.top_k(cand, PEER_TOPK)
    i1 = jnp.take_along_axis(top_i[..., 0, :], best_pos // PEER_TOPK, axis=-1)
    i2 = jnp.take_along_axis(top_i[..., 1, :], best_pos % PEER_TOPK, axis=-1)
    expert = i1 * PEER_NKEYS + i2
    gate = jax.nn.softmax(best_v, axis=-1).astype(x.dtype)
    nb = (B * S) // PEER_TOKEN_BLOCK
    hk = PEER_HEADS * PEER_TOPK
    xb = x.reshape(nb, PEER_TOKEN_BLOCK, D)
    eb = expert.reshape(nb, PEER_TOKEN_BLOCK, hk)
    gb = gate.reshape(nb, PEER_TOKEN_BLOCK, hk)

    def block(args):
        xt, et, gt = args
        u = u_tab[et]
        act = jax.nn.gelu(jnp.einsum('td,tkd->tk', xt, u), approximate=False) * gt
        return jnp.einsum('tk,tkd->td', act, v_tab[et])

    y = lax.map(block, (xb, eb, gb))
    return y.reshape(B, S, D)


def setup_inputs(seed: int = 0) -> dict:
    key = jax.random.key(seed)
    ks = jax.random.split(key, 24)
    nrm = jax.random.normal
    f32 = jnp.float32
    L = DEPTH
    return {
        "x": nrm(ks[0], (BATCH, SEQ, D_MODEL), f32),
        "p": nrm(ks[1], (DEPTH, BATCH, SEQ, PLE_DIM), f32),
        "ln0_g": 1.0 + 0.02 * nrm(ks[2], (D_MODEL,), f32),
        "ln0_b": 0.02 * nrm(ks[3], (D_MODEL,), f32),
        "w_in": nrm(ks[4], (L, D_MODEL, IN_COLS), f32) * D_MODEL ** -0.5,
        "hg_lb": 1.0 + 0.1 * nrm(ks[5], (DEPTH + 1, HG_WIDTH), f32),
        "hg_norm_g": 1.0 + 0.02 * nrm(ks[6], (L, HG_DV), f32),
        "w_hg_branch": nrm(ks[7], (L, HG_WIDTH, D_MODEL), f32) * (BETA * HG_WIDTH ** -0.5),
        "pool_w": nrm(ks[8], (L, POOL_GROUPS, POOL_GROUP, POOL_OUT_GROUP), f32) * (BETA * POOL_GROUP ** -0.5),
        "pool_scale": 1.0 + 0.1 * nrm(ks[9], (L, D_MODEL), f32),
        "w_out": nrm(ks[10], (L, D_MODEL, D_MODEL), f32) * (BETA * D_MODEL ** -0.5),
        "ln1_g": 1.0 + 0.02 * nrm(ks[11], (L, D_MODEL), f32),
        "ln1_b": 0.02 * nrm(ks[12], (L, D_MODEL), f32),
        "w_query": nrm(ks[13], (L, D_MODEL, PEER_HEADS * PEER_QDIM), f32) * D_MODEL ** -0.5,
        "sub_keys": nrm(ks[14], (L, PEER_HEADS, 2, PEER_NKEYS, PEER_HALF), f32) * PEER_HALF ** -0.5,
        "u_tab": nrm(ks[15], (L, PEER_EXPERTS, D_MODEL), f32) * D_MODEL ** -0.5,
        "v_tab": nrm(ks[16], (L, PEER_EXPERTS, D_MODEL), f32) * (BETA * PEER_HEADS ** -0.5),
        "w_ple_gate": nrm(ks[17], (L, D_MODEL, D_MODEL), f32) * D_MODEL ** -0.5,
        "w_ple_proj": nrm(ks[18], (L, PLE_DIM, D_MODEL), f32) * (BETA * PLE_DIM ** -0.5),
        "ln2_g": 1.0 + 0.02 * nrm(ks[19], (L, D_MODEL), f32),
        "ln2_b": 0.02 * nrm(ks[20], (L, D_MODEL), f32),
    }


def reference(x, p, ln0_g, ln0_b, w_in, hg_lb, hg_norm_g, w_hg_branch, pool_w, pool_scale,
              w_out, ln1_g, ln1_b, w_query, sub_keys, u_tab, v_tab, w_ple_gate, w_ple_proj,
              ln2_g, ln2_b):
    h = layer_norm(x, ln0_g, ln0_b)
    lb_all = jnp.cumsum(jax.nn.softmax(hg_lb.astype(jnp.float32), axis=0), axis=0)
    for l in range(DEPTH):
        proj = h @ w_in[l]
        q, f_logit, i_val, g_out, v_pool, gate_a, gate_b = jnp.split(proj, IN_SPLITS, axis=-1)
        y_a = hgrn2_mixer(q, f_logit, i_val, g_out, lb_all[l], hg_norm_g[l]) @ w_hg_branch[l]
        y_b = pool_mixer(v_pool, pool_w[l], pool_scale[l])
        mix = jax.nn.sigmoid(gate_a) * y_a + jax.nn.sigmoid(gate_b) * y_b
        h = layer_norm(ALPHA * h + mix @ w_out[l], ln1_g[l], ln1_b[l])
        ple = jax.nn.sigmoid(h @ w_ple_gate[l]) * (p[l] @ w_ple_proj[l])
        ffn = peer_ffn(h, w_query[l], sub_keys[l], u_tab[l], v_tab[l])
        h = layer_norm(ALPHA * h + ffn + ple, ln2_g[l], ln2_b[l])
    return h
```

```python
import functools

import jax
import jax.numpy as jnp
from jax import lax
from jax.experimental import pallas as pl
from jax.experimental.pallas import tpu as pltpu

F32 = jnp.float32
BF16 = jnp.bfloat16

HG_HEADS = 8
HG_DK = 128
HG_CHUNK = 32
POOL_WINDOWS = (2, 4, 8, 16)
POOL_GROUP = 128
MAX_WIN = 16
PEER_HEADS = 8
PEER_NKEYS = 128
PEER_HALF = 128
PEER_TOPK = 16
LN_EPS = 1e-5
RMS_EPS = 1e-6

LANES = 128
SUBLANES = 8
VMEM_PHYSICAL_BYTES = 64 * 1024 * 1024

MIXER_TOKENS = 256
ROUTER_TOKENS = 256
PEER_TOKENS = 32
PACK_ROWS = 4


def _layer_norm(x, g, b):
    mu = jnp.mean(x, axis=-1, keepdims=True)
    xc = x - mu
    var = jnp.mean(xc * xc, axis=-1, keepdims=True)
    return xc * lax.rsqrt(var + LN_EPS) * g + b


def _sigmoid(x):
    return 1.0 / (1.0 + jnp.exp(-x))


def _dot_nt(a, b):
    return lax.dot_general(a, b, (((1,), (1,)), ((), ())), preferred_element_type=F32)


def _dot_tn(a, b):
    return lax.dot_general(a, b, (((0,), (0,)), ((), ())), preferred_element_type=F32)


def _mixer_kernel(alpha, x_ref, ln0g_ref, ln0b_ref, win_ref, hglb_ref, ng_ref, whg_ref, pw_ref,
                  ps_ref, wout_ref, ln1g_ref, ln1b_ref, h1_ref,
                  st_ref, ext_ref, qd_ref, ki_ref, ke_ref, vv_ref, dec_ref, o_ref, yb_ref):
    ts, d = x_ref.shape
    nc = ts // HG_CHUNK
    width = HG_HEADS * HG_DK
    pool_width = len(POOL_WINDOWS) * POOL_GROUP
    s_idx = pl.program_id(1)

    @pl.when(s_idx == 0)
    def _():
        st_ref[...] = jnp.zeros_like(st_ref)
        ext_ref[0:MAX_WIN, :] = jnp.zeros((MAX_WIN, pool_width), F32)

    h0 = _layer_norm(x_ref[...], ln0g_ref[...], ln0b_ref[...])
    hb = h0.astype(BF16)

    def proj(lo, hi):
        return jnp.dot(hb, win_ref[:, lo:hi], preferred_element_type=F32)

    lbl = hglb_ref[...]
    lmax = jnp.max(lbl, axis=0, keepdims=True)
    lexp = jnp.exp(lbl - lmax)
    lb = lexp[0:1, :] / jnp.sum(lexp, axis=0, keepdims=True)

    f = lb + (1.0 - lb) * _sigmoid(proj(width, 2 * width))
    kk = 1.0 - f
    bh = jnp.log(f)
    row_in_chunk = lax.broadcasted_iota(jnp.int32, (ts, 1), 0) % HG_CHUNK
    sh = 1
    while sh < HG_CHUNK:
        bh = bh + jnp.where(row_in_chunk >= sh, pltpu.roll(bh, sh, axis=0), 0.0)
        sh *= 2
    bh3 = bh.reshape(nc, HG_CHUNK, width)
    bl3 = bh3[:, HG_CHUNK - 1:HG_CHUNK, :]
    dec_ref[...] = jnp.exp(bl3.reshape(nc, width))
    ke_ref[...] = (kk * jnp.exp(jnp.broadcast_to(bl3, bh3.shape).reshape(ts, width) - bh)).astype(BF16)
    ki_ref[...] = (kk * jnp.exp(-bh)).astype(BF16)
    q = proj(0, width)
    qd_ref[...] = (q * _sigmoid(q) * (HG_DK ** -0.5) * jnp.exp(bh)).astype(BF16)
    vv_ref[...] = proj(2 * width, 3 * width).astype(BF16)
    g_out = proj(3 * width, 4 * width)
    g_act = g_out * _sigmoid(g_out)

    r_i = lax.broadcasted_iota(jnp.int32, (ts, ts), 0)
    c_i = lax.broadcasted_iota(jnp.int32, (ts, ts), 1)
    causal = (r_i // HG_CHUNK == c_i // HG_CHUNK) & (c_i <= r_i)

    for h in range(HG_HEADS):
        cols = slice(h * HG_DK, (h + 1) * HG_DK)
        qd_h = qd_ref[:, cols]
        v_h = vv_ref[:, cols]
        scores = jnp.where(causal, _dot_nt(qd_h, ki_ref[:, cols]), 0.0)
        o_h = jnp.dot(scores.astype(BF16), v_h, preferred_element_type=F32)
        st = st_ref[h]
        inter = []
        for c in range(nc):
            rows = slice(c * HG_CHUNK, (c + 1) * HG_CHUNK)
            inter.append(_dot_nt(qd_ref[rows, cols], st.astype(BF16)))
            st = st * dec_ref[c:c + 1, cols] + _dot_tn(vv_ref[rows, cols], ke_ref[rows, cols])
        st_ref[h] = st
        o_h = o_h + jnp.concatenate(inter, axis=0)
        o_h = o_h * lax.rsqrt(jnp.mean(o_h * o_h, axis=-1, keepdims=True) + RMS_EPS) * ng_ref[...]
        o_ref[:, cols] = (o_h * g_act[:, cols]).astype(BF16)
    y_a = jnp.dot(o_ref[...], whg_ref[...], preferred_element_type=F32)

    v_pool = proj(4 * width, 4 * width + pool_width)
    ext_ref[MAX_WIN:MAX_WIN + ts, :] = v_pool
    pos = s_idx * ts + lax.broadcasted_iota(jnp.int32, (ts, 1), 0)
    out_group = d // len(POOL_WINDOWS)
    for gi, w in enumerate(POOL_WINDOWS):
        gcols = slice(gi * POOL_GROUP, (gi + 1) * POOL_GROUP)
        wsum = v_pool[:, gcols]
        for j in range(1, w):
            wsum = wsum + ext_ref[MAX_WIN - j:MAX_WIN - j + ts, gcols]
        cnt = jnp.minimum(pos + 1, w).astype(F32)
        pooled = wsum / cnt - v_pool[:, gcols]
        yb_ref[:, gi * out_group:(gi + 1) * out_group] = jnp.dot(
            pooled.astype(BF16), pw_ref[gi], preferred_element_type=F32)
    ext_ref[0:MAX_WIN, :] = ext_ref[ts:ts + MAX_WIN, :]
    y_b = yb_ref[...] * ps_ref[...]

    gate_a = proj(4 * width + pool_width, 4 * width + pool_width + d)
    gate_b = proj(4 * width + pool_width + d, 4 * width + pool_width + 2 * d)
    mix = _sigmoid(gate_a) * y_a + _sigmoid(gate_b) * y_b
    y = alpha * h0 + jnp.dot(mix.astype(BF16), wout_ref[...], preferred_element_type=F32)
    h1_ref[...] = _layer_norm(y, ln1g_ref[...], ln1b_ref[...])


def _const_spec(shape):
    nd = len(shape)
    return pl.BlockSpec(shape, lambda *_: (0,) * nd, pipeline_mode=pl.Buffered(1))


def _mixer(x2, batch, seq, alpha, ln0_g, ln0_b, w_in, hg_lb, norm_g, w_hg, pool_w, pool_scale,
           w_out, ln1_g, ln1_b):
    t, d = x2.shape
    ts = MIXER_TOKENS
    ns = seq // ts
    width = HG_HEADS * HG_DK
    pool_width = len(POOL_WINDOWS) * POOL_GROUP
    row = lambda a: a.reshape(1, -1)
    args = (x2, row(ln0_g), row(ln0_b), w_in.astype(BF16), hg_lb, row(norm_g), w_hg.astype(BF16),
            pool_w.astype(BF16), row(pool_scale), w_out.astype(BF16), row(ln1_g), row(ln1_b))
    in_specs = [pl.BlockSpec((ts, d), lambda b, s: (b * ns + s, 0))]
    in_specs += [_const_spec(a.shape) for a in args[1:]]
    return pl.pallas_call(
        functools.partial(_mixer_kernel, alpha),
        grid=(batch, ns),
        in_specs=in_specs,
        out_specs=pl.BlockSpec((ts, d), lambda b, s: (b * ns + s, 0)),
        out_shape=jax.ShapeDtypeStruct((t, d), F32),
        scratch_shapes=[
            pltpu.VMEM((HG_HEADS, HG_DK, HG_DK), F32),
            pltpu.VMEM((ts + MAX_WIN, pool_width), F32),
            pltpu.VMEM((ts, width), BF16),
            pltpu.VMEM((ts, width), BF16),
            pltpu.VMEM((ts, width), BF16),
            pltpu.VMEM((ts, width), BF16),
            pltpu.VMEM((ts // HG_CHUNK, width), F32),
            pltpu.VMEM((ts, width), BF16),
            pltpu.VMEM((ts, d), F32),
        ],
        compiler_params=pltpu.CompilerParams(
            dimension_semantics=("arbitrary", "arbitrary"),
            vmem_limit_bytes=48 * 1024 * 1024),
        name="mixer",
    )(*args)


def _topk_rows(sc, k):
    n = sc.shape[0]
    iota = lax.broadcasted_iota(jnp.int32, sc.shape, 0)
    vals, idxs = [], []
    for _ in range(k):
        m = jnp.max(sc, axis=0, keepdims=True)
        idx = jnp.min(jnp.where(sc == m, iota, n), axis=0, keepdims=True)
        vals.append(m)
        idxs.append(idx)
        sc = jnp.where(iota == idx, -jnp.inf, sc)
    return vals, idxs


def _router_kernel(alpha, h_ref, p_ref, wq_ref, keys_ref, wpg_ref, wpp_ref, ids_ref, gates_ref, base_ref):
    h = h_ref[...]
    hb = h.astype(BF16)
    tb = h.shape[0]
    ple = _sigmoid(jnp.dot(hb, wpg_ref[...], preferred_element_type=F32)) * jnp.dot(
        p_ref[...].astype(BF16), wpp_ref[...], preferred_element_type=F32)
    base_ref[...] = alpha * h + ple

    q = jnp.dot(hb, wq_ref[...], preferred_element_type=F32).astype(BF16)
    k = PEER_TOPK
    for hd in range(PEER_HEADS):
        tv, ti = [], []
        for half in range(2):
            lo = (hd * 2 + half) * PEER_HALF
            sc_t = _dot_nt(keys_ref[hd, half], q[:, lo:lo + PEER_HALF])
            v, i = _topk_rows(sc_t, k)
            tv.append(v)
            ti.append(i)
        cand = jnp.concatenate([tv[0][a] + jnp.concatenate(tv[1], axis=0) for a in range(k)], axis=0)
        cand_id = jnp.concatenate(
            [ti[0][a] * PEER_NKEYS + jnp.concatenate(ti[1], axis=0) for a in range(k)], axis=0)
        n = cand.shape[0]
        iota = lax.broadcasted_iota(jnp.int32, (n, tb), 0)
        best_v, best_e = [], []
        for _ in range(k):
            m = jnp.max(cand, axis=0, keepdims=True)
            pos = jnp.min(jnp.where(cand == m, iota, n), axis=0, keepdims=True)
            hit = iota == pos
            best_v.append(m)
            best_e.append(jnp.max(jnp.where(hit, cand_id, -1), axis=0, keepdims=True))
            cand = jnp.where(hit, -jnp.inf, cand)
        bv = jnp.concatenate(best_v, axis=0)
        ex = jnp.exp(bv - bv[0:1, :])
        gates_ref[hd * k:(hd + 1) * k, :] = ex / jnp.sum(ex, axis=0, keepdims=True)
        ids_ref[hd * k:(hd + 1) * k, :] = jnp.concatenate(best_e, axis=0)


def _router(h1, p2, alpha, w_query, sub_keys, w_ple_gate, w_ple_proj):
    t, d = h1.shape
    tb = ROUTER_TOKENS
    slots = PEER_HEADS * PEER_TOPK
    args = (h1, p2, w_query.astype(BF16), sub_keys.astype(BF16), w_ple_gate.astype(BF16),
            w_ple_proj.astype(BF16))
    in_specs = [pl.BlockSpec((tb, d), lambda i: (i, 0)),
                pl.BlockSpec((tb, p2.shape[1]), lambda i: (i, 0))]
    in_specs += [_const_spec(a.shape) for a in args[2:]]
    return pl.pallas_call(
        functools.partial(_router_kernel, alpha),
        grid=(t // tb,),
        in_specs=in_specs,
        out_specs=[pl.BlockSpec((slots, tb), lambda i: (0, i)),
                   pl.BlockSpec((slots, tb), lambda i: (0, i)),
                   pl.BlockSpec((tb, d), lambda i: (i, 0))],
        out_shape=[jax.ShapeDtypeStruct((slots, t), jnp.int32),
                   jax.ShapeDtypeStruct((slots, t), F32),
                   jax.ShapeDtypeStruct((t, d), F32)],
        compiler_params=pltpu.CompilerParams(
            dimension_semantics=("arbitrary",),
            vmem_limit_bytes=48 * 1024 * 1024),
        name="router",
    )(*args)


def _pack_table(tab):
    n, d = tab.shape
    bits = lax.bitcast_convert_type(tab.astype(BF16).reshape(n, 2, PACK_ROWS, LANES), jnp.uint16)
    bits = bits.astype(jnp.uint32)
    return (bits[:, 0] | (bits[:, 1] << 16)).reshape(n * PACK_ROWS, LANES)


def _gather_rows(ids_ref, tab_ref, g_ref, t, slots):
    for k in range(slots):
        e = ids_ref[t * slots + k]
        slab = tab_ref[pl.ds(pl.multiple_of(e * PACK_ROWS, PACK_ROWS), PACK_ROWS), :]
        start = (k // SUBLANES) * (PACK_ROWS * SUBLANES) + k % SUBLANES
        g_ref[t, pl.ds(start, PACK_ROWS, stride=SUBLANES), :] = slab


def _gathered_matrix(g_ref, t, slots):
    rows = []
    for kt in range(slots // SUBLANES):
        tiles = [g_ref[t, (kt * PACK_ROWS + c) * SUBLANES:(kt * PACK_ROWS + c + 1) * SUBLANES, :]
                 for c in range(PACK_ROWS)]
        rows.append(jnp.concatenate(tiles, axis=1))
    return pltpu.bitcast(jnp.concatenate(rows, axis=0), BF16)


def _pair_select(slots):
    r = lax.broadcasted_iota(jnp.int32, (2 * slots, slots), 0)
    c = lax.broadcasted_iota(jnp.int32, (2 * slots, slots), 1)
    return (r // 2 == c).astype(F32)


def _gelu(x):
    return 0.5 * x * (1.0 + lax.erf(x * (2.0 ** -0.5)))


def _peer_u_kernel(ids_ref, h_ref, gate_ref, tab_ref, act_ref, g_ref, xs_ref, r_ref):
    tb, d = h_ref.shape
    slots = gate_ref.shape[1]
    half = d // 2

    def gather(t, carry):
        _gather_rows(ids_ref, tab_ref, g_ref, t, slots)
        return carry

    lax.fori_loop(0, tb, gather, 0)

    x = h_ref[...]
    x_hi = x.astype(BF16).astype(F32)
    xs_ref[0] = x_hi
    xs_ref[1] = x - x_hi
    lane_even = lax.broadcasted_iota(jnp.int32, (1, 2 * slots), 1) % 2 == 0

    def dots(t, carry):
        gm = _gathered_matrix(g_ref, t, slots)
        xh = xs_ref[0, pl.ds(t, 1), :]
        xl = xs_ref[1, pl.ds(t, 1), :]
        lhs = jnp.concatenate([xh[:, :half], xl[:, :half], xh[:, half:], xl[:, half:],
                               jnp.zeros((SUBLANES - 4, half), F32)], axis=0).astype(BF16)
        r = _dot_nt(lhs, gm)
        r_ref[pl.ds(t, 1), :] = jnp.where(lane_even, r[0:1] + r[1:2], r[2:3] + r[3:4])
        return carry

    lax.fori_loop(0, tb, dots, 0)
    pre = jnp.dot(r_ref[...], _pair_select(slots), preferred_element_type=F32,
                  precision=lax.Precision.HIGHEST)
    act_ref[...] = _gelu(pre) * gate_ref[...]


def _peer_v_kernel(ids_ref, act_ref, base_ref, tab_ref, lng_ref, lnb_ref, out_ref, g_ref, as_ref, o_ref):
    tb, d = base_ref.shape
    slots = act_ref.shape[1]
    half = d // 2

    def gather(t, carry):
        _gather_rows(ids_ref, tab_ref, g_ref, t, slots)
        return carry

    lax.fori_loop(0, tb, gather, 0)

    a2 = lax.dot_general(act_ref[...], _pair_select(slots), (((1,), (1,)), ((), ())),
                         preferred_element_type=F32, precision=lax.Precision.HIGHEST)
    lane_even = lax.broadcasted_iota(jnp.int32, (1, 2 * slots), 1) % 2 == 0
    a_hi = a2.astype(BF16).astype(F32)
    a_lo = a2 - a_hi
    as_ref[0] = jnp.where(lane_even, a_hi, 0.0)
    as_ref[1] = jnp.where(lane_even, a_lo, 0.0)
    as_ref[2] = jnp.where(lane_even, 0.0, a_hi)
    as_ref[3] = jnp.where(lane_even, 0.0, a_lo)

    def dots(t, carry):
        gm = _gathered_matrix(g_ref, t, slots)
        lhs = jnp.concatenate([as_ref[i, pl.ds(t, 1), :] for i in range(4)]
                              + [jnp.zeros((SUBLANES - 4, 2 * slots), F32)], axis=0).astype(BF16)
        r = jnp.dot(lhs, gm, preferred_element_type=F32)
        o_ref[pl.ds(t, 1), 0:half] = r[0:1] + r[1:2]
        o_ref[pl.ds(t, 1), half:d] = r[2:3] + r[3:4]
        return carry

    lax.fori_loop(0, tb, dots, 0)
    out_ref[...] = _layer_norm(base_ref[...] + o_ref[...], lng_ref[...], lnb_ref[...])


def _peer_specs(tb, d, slots, tab):
    ids_spec = pl.BlockSpec((tb * slots,), lambda i: (i,), memory_space=pltpu.SMEM)
    tile_spec = pl.BlockSpec((tb, d), lambda i: (i, 0))
    slot_spec = pl.BlockSpec((tb, slots), lambda i: (i, 0))
    return ids_spec, tile_spec, slot_spec, _const_spec(tab.shape)


def _peer_u(ids_flat, h1, gates, tab):
    t, d = h1.shape
    slots = gates.shape[1]
    tb = PEER_TOKENS
    ids_spec, tile_spec, slot_spec, tab_spec = _peer_specs(tb, d, slots, tab)
    return pl.pallas_call(
        _peer_u_kernel,
        grid=(t // tb,),
        in_specs=[ids_spec, tile_spec, slot_spec, tab_spec],
        out_specs=slot_spec,
        out_shape=jax.ShapeDtypeStruct((t, slots), F32),
        scratch_shapes=[
            pltpu.VMEM((tb, slots * PACK_ROWS, LANES), jnp.uint32),
            pltpu.VMEM((2, tb, d), F32),
            pltpu.VMEM((tb, 2 * slots), F32),
        ],
        compiler_params=pltpu.CompilerParams(
            dimension_semantics=("arbitrary",),
            vmem_limit_bytes=52 * 1024 * 1024),
        name="peer_u",
    )(ids_flat, h1, gates, tab)


def _peer_v(ids_flat, act, base, tab, ln_g, ln_b):
    t, d = base.shape
    slots = act.shape[1]
    tb = PEER_TOKENS
    ids_spec, tile_spec, slot_spec, tab_spec = _peer_specs(tb, d, slots, tab)
    row = lambda a: a.reshape(1, -1)
    return pl.pallas_call(
        _peer_v_kernel,
        grid=(t // tb,),
        in_specs=[ids_spec, slot_spec, tile_spec, tab_spec, _const_spec((1, d)), _const_spec((1, d))],
        out_specs=tile_spec,
        out_shape=jax.ShapeDtypeStruct((t, d), F32),
        scratch_shapes=[
            pltpu.VMEM((tb, slots * PACK_ROWS, LANES), jnp.uint32),
            pltpu.VMEM((4, tb, 2 * slots), F32),
            pltpu.VMEM((tb, d), F32),
        ],
        compiler_params=pltpu.CompilerParams(
            dimension_semantics=("arbitrary",),
            vmem_limit_bytes=52 * 1024 * 1024),
        name="peer_v",
    )(ids_flat, act, base, tab, row(ln_g), row(ln_b))


def kernel(x, p, ln0_g, ln0_b, w_in, hg_lb, hg_norm_g, w_hg_branch, pool_w, pool_scale, w_out,
           ln1_g, ln1_b, w_query, sub_keys, u_tab, v_tab, w_ple_gate, w_ple_proj, ln2_g, ln2_b):
    batch, seq, d = x.shape
    depth = w_in.shape[0]
    assert depth == 1 and seq % MIXER_TOKENS == 0 and (batch * seq) % ROUTER_TOKENS == 0
    alpha = (2.0 * depth) ** 0.25
    t = batch * seq
    h1 = _mixer(x.reshape(t, d), batch, seq, alpha, ln0_g, ln0_b, w_in[0], hg_lb, hg_norm_g[0],
                w_hg_branch[0], pool_w[0], pool_scale[0], w_out[0], ln1_g[0], ln1_b[0])
    ids_t, gates_t, base = _router(h1, p[0].reshape(t, -1), alpha, w_query[0], sub_keys[0],
                                   w_ple_gate[0], w_ple_proj[0])
    ids_flat = ids_t.T.reshape(-1)
    act = _peer_u(ids_flat, h1, gates_t.T, _pack_table(u_tab[0]))
    out = _peer_v(ids_flat, act, base, _pack_table(v_tab[0]), ln2_g[0], ln2_b[0])
    return out.reshape(batch, seq, d)
```

```python
import functools

import jax
import jax.numpy as jnp
from jax import lax
from jax.experimental import pallas as pl
from jax.experimental.pallas import tpu as pltpu

F32 = jnp.float32
BF16 = jnp.bfloat16

HG_HEADS = 8
HG_DK = 128
HG_CHUNK = 32
POOL_WINDOWS = (2, 4, 8, 16)
POOL_GROUP = 128
MAX_WIN = 16
PEER_HEADS = 8
PEER_NKEYS = 128
PEER_HALF = 128
PEER_TOPK = 16
LN_EPS = 1e-5
RMS_EPS = 1e-6

LANES = 128
SUBLANES = 8

MIXER_TOKENS = 256
ROUTER_TOKENS = 256
PEER_TOKENS = 64
PACK_ROWS = 4


def _layer_norm(x, g, b):
    mu = jnp.mean(x, axis=-1, keepdims=True)
    xc = x - mu
    var = jnp.mean(xc * xc, axis=-1, keepdims=True)
    return xc * lax.rsqrt(var + LN_EPS) * g + b


def _sigmoid(x):
    return 1.0 / (1.0 + jnp.exp(-x))


def _dot_nt(a, b):
    return lax.dot_general(a, b, (((1,), (1,)), ((), ())), preferred_element_type=F32)


def _dot_tn(a, b):
    return lax.dot_general(a, b, (((0,), (0,)), ((), ())), preferred_element_type=F32)


def _mixer_kernel(alpha, x_ref, ln0g_ref, ln0b_ref, win_ref, hglb_ref, ng_ref, whg_ref, pw_ref,
                  ps_ref, wout_ref, ln1g_ref, ln1b_ref, h1_ref,
                  st_ref, ext_ref, qd_ref, ki_ref, ke_ref, vv_ref, dec_ref, o_ref, yb_ref):
    ts, d = x_ref.shape
    nc = ts // HG_CHUNK
    width = HG_HEADS * HG_DK
    pool_width = len(POOL_WINDOWS) * POOL_GROUP
    s_idx = pl.program_id(1)

    @pl.when(s_idx == 0)
    def _():
        st_ref[...] = jnp.zeros_like(st_ref)
        ext_ref[0:MAX_WIN, :] = jnp.zeros((MAX_WIN, pool_width), F32)

    h0 = _layer_norm(x_ref[...], ln0g_ref[...], ln0b_ref[...])
    hb = h0.astype(BF16)

    def proj(lo, hi):
        return jnp.dot(hb, win_ref[:, lo:hi], preferred_element_type=F32)

    lbl = hglb_ref[...]
    lmax = jnp.max(lbl, axis=0, keepdims=True)
    lexp = jnp.exp(lbl - lmax)
    lb = lexp[0:1, :] / jnp.sum(lexp, axis=0, keepdims=True)

    f = lb + (1.0 - lb) * _sigmoid(proj(width, 2 * width))
    kk = 1.0 - f
    bh = jnp.log(f)
    row_in_chunk = lax.broadcasted_iota(jnp.int32, (ts, 1), 0) % HG_CHUNK
    sh = 1
    while sh < HG_CHUNK:
        bh = bh + jnp.where(row_in_chunk >= sh, pltpu.roll(bh, sh, axis=0), 0.0)
        sh *= 2
    bh3 = bh.reshape(nc, HG_CHUNK, width)
    bl3 = bh3[:, HG_CHUNK - 1:HG_CHUNK, :]
    dec_ref[...] = jnp.exp(bl3.reshape(nc, width))
    ke_ref[...] = (kk * jnp.exp(jnp.broadcast_to(bl3, bh3.shape).reshape(ts, width) - bh)).astype(BF16)
    ki_ref[...] = (kk * jnp.exp(-bh)).astype(BF16)
    q = proj(0, width)
    qd_ref[...] = (q * _sigmoid(q) * (HG_DK ** -0.5) * jnp.exp(bh)).astype(BF16)
    vv_ref[...] = proj(2 * width, 3 * width).astype(BF16)
    g_out = proj(3 * width, 4 * width)
    g_act = g_out * _sigmoid(g_out)

    r_i = lax.broadcasted_iota(jnp.int32, (ts, ts), 0)
    c_i = lax.broadcasted_iota(jnp.int32, (ts, ts), 1)
    causal = (r_i // HG_CHUNK == c_i // HG_CHUNK) & (c_i <= r_i)

    for h in range(HG_HEADS):
        cols = slice(h * HG_DK, (h + 1) * HG_DK)
        qd_h = qd_ref[:, cols]
        v_h = vv_ref[:, cols]
        scores = jnp.where(causal, _dot_nt(qd_h, ki_ref[:, cols]), 0.0)
        o_h = jnp.dot(scores.astype(BF16), v_h, preferred_element_type=F32)
        st = st_ref[h]
        inter = []
        for c in range(nc):
            rows = slice(c * HG_CHUNK, (c + 1) * HG_CHUNK)
            inter.append(_dot_nt(qd_ref[rows, cols], st.astype(BF16)))
            st = st * dec_ref[c:c + 1, cols] + _dot_tn(vv_ref[rows, cols], ke_ref[rows, cols])
        st_ref[h] = st
        o_h = o_h + jnp.concatenate(inter, axis=0)
        o_h = o_h * lax.rsqrt(jnp.mean(o_h * o_h, axis=-1, keepdims=True) + RMS_EPS) * ng_ref[...]
        o_ref[:, cols] = (o_h * g_act[:, cols]).astype(BF16)
    y_a = jnp.dot(o_ref[...], whg_ref[...], preferred_element_type=F32)

    v_pool = proj(4 * width, 4 * width + pool_width)
    ext_ref[MAX_WIN:MAX_WIN + ts, :] = v_pool
    pos = s_idx * ts + lax.broadcasted_iota(jnp.int32, (ts, 1), 0)
    out_group = d // len(POOL_WINDOWS)
    for gi, w in enumerate(POOL_WINDOWS):
        gcols = slice(gi * POOL_GROUP, (gi + 1) * POOL_GROUP)
        wsum = v_pool[:, gcols]
        for j in range(1, w):
            wsum = wsum + ext_ref[MAX_WIN - j:MAX_WIN - j + ts, gcols]
        cnt = jnp.minimum(pos + 1, w).astype(F32)
        pooled = wsum / cnt - v_pool[:, gcols]
        yb_ref[:, gi * out_group:(gi + 1) * out_group] = jnp.dot(
            pooled.astype(BF16), pw_ref[gi], preferred_element_type=F32)
    ext_ref[0:MAX_WIN, :] = ext_ref[ts:ts + MAX_WIN, :]
    y_b = yb_ref[...] * ps_ref[...]

    gate_a = proj(4 * width + pool_width, 4 * width + pool_width + d)
    gate_b = proj(4 * width + pool_width + d, 4 * width + pool_width + 2 * d)
    mix = _sigmoid(gate_a) * y_a + _sigmoid(gate_b) * y_b
    y = alpha * h0 + jnp.dot(mix.astype(BF16), wout_ref[...], preferred_element_type=F32)
    h1_ref[...] = _layer_norm(y, ln1g_ref[...], ln1b_ref[...])


def _const_spec(shape):
    nd = len(shape)
    return pl.BlockSpec(shape, lambda *_: (0,) * nd, pipeline_mode=pl.Buffered(1))


def _mixer(x2, batch, seq, alpha, ln0_g, ln0_b, w_in, hg_lb, norm_g, w_hg, pool_w, pool_scale,
           w_out, ln1_g, ln1_b):
    t, d = x2.shape
    ts = MIXER_TOKENS
    ns = seq // ts
    width = HG_HEADS * HG_DK
    pool_width = len(POOL_WINDOWS) * POOL_GROUP
    row = lambda a: a.reshape(1, -1)
    args = (x2, row(ln0_g), row(ln0_b), w_in.astype(BF16), hg_lb, row(norm_g), w_hg.astype(BF16),
            pool_w.astype(BF16), row(pool_scale), w_out.astype(BF16), row(ln1_g), row(ln1_b))
    in_specs = [pl.BlockSpec((ts, d), lambda b, s: (b * ns + s, 0))]
    in_specs += [_const_spec(a.shape) for a in args[1:]]
    return pl.pallas_call(
        functools.partial(_mixer_kernel, alpha),
        grid=(batch, ns),
        in_specs=in_specs,
        out_specs=pl.BlockSpec((ts, d), lambda b, s: (b * ns + s, 0)),
        out_shape=jax.ShapeDtypeStruct((t, d), F32),
        scratch_shapes=[
            pltpu.VMEM((HG_HEADS, HG_DK, HG_DK), F32),
            pltpu.VMEM((ts + MAX_WIN, pool_width), F32),
            pltpu.VMEM((ts, width), BF16),
            pltpu.VMEM((ts, width), BF16),
            pltpu.VMEM((ts, width), BF16),
            pltpu.VMEM((ts, width), BF16),
            pltpu.VMEM((ts // HG_CHUNK, width), F32),
            pltpu.VMEM((ts, width), BF16),
            pltpu.VMEM((ts, d), F32),
        ],
        compiler_params=pltpu.CompilerParams(
            dimension_semantics=("arbitrary", "arbitrary"),
            vmem_limit_bytes=48 * 1024 * 1024),
        name="mixer",
    )(*args)


def _extract_max(vals, pos, payloads):
    m = jnp.max(vals, axis=0, keepdims=True)
    big = jnp.iinfo(jnp.int32).max
    sel = jnp.min(jnp.where(vals == m, pos, big), axis=0, keepdims=True)
    hit = pos == sel
    picked = [jnp.max(jnp.where(hit, pay, -1), axis=0, keepdims=True) for pay in payloads]
    return m, sel, picked, jnp.where(hit, -jnp.inf, vals)


def _router_kernel(alpha, h_ref, p_ref, wq_ref, keys_ref, wpg_ref, wpp_ref, ids_ref, gates_ref, base_ref,
                   q_ref):
    h = h_ref[...]
    hb = h.astype(BF16)
    tb = h.shape[0]
    k = PEER_TOPK
    ple = _sigmoid(jnp.dot(hb, wpg_ref[...], preferred_element_type=F32)) * jnp.dot(
        p_ref[...].astype(BF16), wpp_ref[...], preferred_element_type=F32)
    base_ref[...] = alpha * h + ple

    q = jnp.dot(hb, wq_ref[...], preferred_element_type=F32).astype(BF16)
    for hp in range(2 * PEER_HEADS):
        q_ref[hp] = q[:, hp * PEER_HALF:(hp + 1) * PEER_HALF]

    key_iota = lax.broadcasted_iota(jnp.int32, (PEER_NKEYS, LANES), 0)
    row8 = lax.broadcasted_iota(jnp.int32, (SUBLANES, LANES), 0)
    row16 = lax.broadcasted_iota(jnp.int32, (k, LANES), 0)

    def head(hd, carry):
        for grp in range(tb // LANES):
            tok = slice(grp * LANES, (grp + 1) * LANES)
            tv, ti = [], []
            for half in range(2):
                sc = _dot_nt(keys_ref[hd, half], q_ref[2 * hd + half, tok, :])
                vals, idxs = [], []
                for _ in range(k):
                    m, sel, _, sc = _extract_max(sc, key_iota, [])
                    vals.append(m)
                    idxs.append(sel)
                tv.append(vals)
                ti.append(idxs)
            tv2 = jnp.concatenate(tv[1], axis=0)
            ti2 = jnp.concatenate(ti[1], axis=0)
            cands, poss, cids = [tv[0][0] + tv2], [row16], [ti[0][0] * PEER_NKEYS + ti2]
            for a in range(1, SUBLANES):
                keep = (row8 + 1) * (a + 1) <= k
                cands.append(jnp.where(keep, tv[0][a] + tv2[0:SUBLANES], -jnp.inf))
                poss.append(row8 + a * k)
                cids.append(ti[0][a] * PEER_NKEYS + ti2[0:SUBLANES])
            cands.append(jnp.concatenate(tv[0][SUBLANES:], axis=0) + tv2[0:1])
            poss.append((row8 + SUBLANES) * k)
            cids.append(jnp.concatenate(ti[0][SUBLANES:], axis=0) * PEER_NKEYS + ti2[0:1])
            cand = jnp.concatenate(cands, axis=0)
            pos = jnp.concatenate(poss, axis=0)
            cid = jnp.concatenate(cids, axis=0)
            best_v, best_e = [], []
            for _ in range(k):
                m, _, (e,), cand = _extract_max(cand, pos, [cid])
                best_v.append(m)
                best_e.append(e)
            bv = jnp.concatenate(best_v, axis=0)
            ex = jnp.exp(bv - bv[0:1, :])
            rows = pl.ds(pl.multiple_of(hd * k, k), k)
            gates_ref[rows, tok] = ex / jnp.sum(ex, axis=0, keepdims=True)
            ids_ref[rows, tok] = jnp.concatenate(best_e, axis=0) * PACK_ROWS
        return carry

    lax.fori_loop(0, PEER_HEADS, head, 0)


def _router(h1, p2, alpha, w_query, sub_keys, w_ple_gate, w_ple_proj):
    t, d = h1.shape
    tb = ROUTER_TOKENS
    slots = PEER_HEADS * PEER_TOPK
    args = (h1, p2, w_query.astype(BF16), sub_keys.astype(BF16), w_ple_gate.astype(BF16),
            w_ple_proj.astype(BF16))
    in_specs = [pl.BlockSpec((tb, d), lambda i: (i, 0)),
                pl.BlockSpec((tb, p2.shape[1]), lambda i: (i, 0))]
    in_specs += [_const_spec(a.shape) for a in args[2:]]
    return pl.pallas_call(
        functools.partial(_router_kernel, alpha),
        grid=(t // tb,),
        in_specs=in_specs,
        out_specs=[pl.BlockSpec((slots, tb), lambda i: (0, i)),
                   pl.BlockSpec((slots, tb), lambda i: (0, i)),
                   pl.BlockSpec((tb, d), lambda i: (i, 0))],
        out_shape=[jax.ShapeDtypeStruct((slots, t), jnp.int32),
                   jax.ShapeDtypeStruct((slots, t), F32),
                   jax.ShapeDtypeStruct((t, d), F32)],
        scratch_shapes=[pltpu.VMEM((2 * PEER_HEADS, tb, PEER_HALF), BF16)],
        compiler_params=pltpu.CompilerParams(
            dimension_semantics=("arbitrary",),
            vmem_limit_bytes=48 * 1024 * 1024),
        name="router",
    )(*args)


def _pack_table(tab):
    n, d = tab.shape
    bits = lax.bitcast_convert_type(tab.astype(BF16).reshape(n, 2, PACK_ROWS, LANES), jnp.uint16)
    bits = bits.astype(jnp.uint32)
    return (bits[:, 0] | (bits[:, 1] << 16)).reshape(n * PACK_ROWS, LANES)


def _gather_rows(ids_ref, tab_ref, g_ref, t, slots):
    ids_t = ids_ref.at[pl.ds(t * slots, slots)]
    for k in range(slots):
        slab = tab_ref[pl.ds(pl.multiple_of(ids_t[k], PACK_ROWS), PACK_ROWS), :]
        start = (k // SUBLANES) * (PACK_ROWS * SUBLANES) + k % SUBLANES
        g_ref[pl.ds(start, PACK_ROWS, stride=SUBLANES), :] = slab


def _gathered_matrix(g_ref, slots):
    rows = []
    for kt in range(slots // SUBLANES):
        tiles = [g_ref[(kt * PACK_ROWS + c) * SUBLANES:(kt * PACK_ROWS + c + 1) * SUBLANES, :]
                 for c in range(PACK_ROWS)]
        rows.append(jnp.concatenate(tiles, axis=1))
    return pltpu.bitcast(jnp.concatenate(rows, axis=0), BF16)


def _pipelined_tokens(tb, gather, compute, g0, g1):
    gather(0, g0)
    gather(1, g1)

    def pair(i, carry):
        t = 2 * i
        compute(t, g0)
        compute(t + 1, g1)
        gather(t + 2, g0)
        gather(t + 3, g1)
        return carry

    lax.fori_loop(0, tb // 2 - 1, pair, 0)
    compute(tb - 2, g0)
    compute(tb - 1, g1)


def _pair_select(slots):
    r = lax.broadcasted_iota(jnp.int32, (2 * slots, slots), 0)
    c = lax.broadcasted_iota(jnp.int32, (2 * slots, slots), 1)
    return (r // 2 == c).astype(F32)


def _gelu(x):
    return 0.5 * x * (1.0 + lax.erf(x * (2.0 ** -0.5)))


def _split_bf16(x):
    head = x.astype(BF16).astype(F32)
    return head, x - head


def _store_rows(ref, j, tb, value):
    for c in range(ref.shape[0]):
        ref[c, pl.ds(j, tb, stride=SUBLANES), :] = value[:, c * LANES:(c + 1) * LANES]


def _load_rows(ref, j, tb):
    return jnp.concatenate([ref[c, pl.ds(j, tb, stride=SUBLANES), :] for c in range(ref.shape[0])], axis=1)


def _load_group(ref, rows):
    return jnp.concatenate([ref[c, rows, :] for c in range(ref.shape[0])], axis=1)


def _store_group(ref, rows, value):
    for c in range(ref.shape[0]):
        ref[c, rows, :] = value[:, c * LANES:(c + 1) * LANES]


def _peer_u_kernel(ids_ref, h_ref, gate_ref, tab_ref, act_ref, g0_ref, g1_ref, lhs_ref, r_ref):
    tb, d = h_ref.shape
    slots = gate_ref.shape[1]
    half = d // 2
    parts = 4

    @pl.when(pl.program_id(0) == 0)
    def _():
        lhs_ref[...] = jnp.zeros_like(lhs_ref)

    x_hi, x_lo = _split_bf16(h_ref[...])
    for j, part in enumerate((x_hi[:, :half], x_lo[:, :half], x_hi[:, half:], x_lo[:, half:])):
        _store_rows(lhs_ref, j, tb, part)

    def gather(t, g_ref):
        _gather_rows(ids_ref, tab_ref, g_ref, t, slots)

    def compute(t, g_ref):
        rows = pl.ds(pl.multiple_of(t * SUBLANES, SUBLANES), SUBLANES)
        _store_group(r_ref, rows, _dot_nt(_load_group(lhs_ref, rows).astype(BF16), _gathered_matrix(g_ref, slots)))

    _pipelined_tokens(tb, gather, compute, g0_ref, g1_ref)

    r = [_load_rows(r_ref, j, tb) for j in range(parts)]
    lane_even = lax.broadcasted_iota(jnp.int32, (1, 2 * slots), 1) % 2 == 0
    inter = jnp.where(lane_even, r[0] + r[1], r[2] + r[3])
    pre = jnp.dot(inter, _pair_select(slots), preferred_element_type=F32, precision=lax.Precision.HIGHEST)
    act_ref[...] = _gelu(pre) * gate_ref[...]


def _peer_v_kernel(ids_ref, act_ref, base_ref, tab_ref, lng_ref, lnb_ref, out_ref, g0_ref, g1_ref, lhs_ref,
                   r_ref):
    tb, d = base_ref.shape
    slots = act_ref.shape[1]

    @pl.when(pl.program_id(0) == 0)
    def _():
        lhs_ref[...] = jnp.zeros_like(lhs_ref)

    a2 = lax.dot_general(act_ref[...], _pair_select(slots), (((1,), (1,)), ((), ())),
                         preferred_element_type=F32, precision=lax.Precision.HIGHEST)
    lane_even = lax.broadcasted_iota(jnp.int32, (1, 2 * slots), 1) % 2 == 0
    a_hi, a_lo = _split_bf16(a2)
    for j, part in enumerate((jnp.where(lane_even, a_hi, 0.0), jnp.where(lane_even, a_lo, 0.0),
                              jnp.where(lane_even, 0.0, a_hi), jnp.where(lane_even, 0.0, a_lo))):
        _store_rows(lhs_ref, j, tb, part)

    def gather(t, g_ref):
        _gather_rows(ids_ref, tab_ref, g_ref, t, slots)

    def compute(t, g_ref):
        rows = pl.ds(pl.multiple_of(t * SUBLANES, SUBLANES), SUBLANES)
        _store_group(r_ref, rows, jnp.dot(_load_group(lhs_ref, rows).astype(BF16),
                                          _gathered_matrix(g_ref, slots), preferred_element_type=F32))

    _pipelined_tokens(tb, gather, compute, g0_ref, g1_ref)

    r = [_load_rows(r_ref, j, tb) for j in range(4)]
    ffn = jnp.concatenate([r[0] + r[1], r[2] + r[3]], axis=1)
    out_ref[...] = _layer_norm(base_ref[...] + ffn, lng_ref[...], lnb_ref[...])


def _peer_specs(tb, d, slots, tab):
    ids_spec = pl.BlockSpec((tb * slots,), lambda i: (i,), memory_space=pltpu.SMEM)
    tile_spec = pl.BlockSpec((tb, d), lambda i: (i, 0))
    slot_spec = pl.BlockSpec((tb, slots), lambda i: (i, 0))
    return ids_spec, tile_spec, slot_spec, _const_spec(tab.shape)


def _peer_u(ids_flat, h1, gates, tab):
    t, d = h1.shape
    slots = gates.shape[1]
    tb = PEER_TOKENS
    ids_spec, tile_spec, slot_spec, tab_spec = _peer_specs(tb, d, slots, tab)
    return pl.pallas_call(
        _peer_u_kernel,
        grid=(t // tb,),
        in_specs=[ids_spec, tile_spec, slot_spec, tab_spec],
        out_specs=slot_spec,
        out_shape=jax.ShapeDtypeStruct((t, slots), F32),
        scratch_shapes=[
            pltpu.VMEM((slots * PACK_ROWS, LANES), jnp.uint32),
            pltpu.VMEM((slots * PACK_ROWS, LANES), jnp.uint32),
            pltpu.VMEM((d // 2 // LANES, tb * SUBLANES, LANES), F32),
            pltpu.VMEM((2 * slots // LANES, tb * SUBLANES, LANES), F32),
        ],
        compiler_params=pltpu.CompilerParams(
            dimension_semantics=("arbitrary",),
            vmem_limit_bytes=52 * 1024 * 1024),
        name="peer_u",
    )(ids_flat, h1, gates, tab)


def _peer_v(ids_flat, act, base, tab, ln_g, ln_b):
    t, d = base.shape
    slots = act.shape[1]
    tb = PEER_TOKENS
    ids_spec, tile_spec, slot_spec, tab_spec = _peer_specs(tb, d, slots, tab)
    row = lambda a: a.reshape(1, -1)
    return pl.pallas_call(
        _peer_v_kernel,
        grid=(t // tb,),
        in_specs=[ids_spec, slot_spec, tile_spec, tab_spec, _const_spec((1, d)), _const_spec((1, d))],
        out_specs=tile_spec,
        out_shape=jax.ShapeDtypeStruct((t, d), F32),
        scratch_shapes=[
            pltpu.VMEM((slots * PACK_ROWS, LANES), jnp.uint32),
            pltpu.VMEM((slots * PACK_ROWS, LANES), jnp.uint32),
            pltpu.VMEM((2 * slots // LANES, tb * SUBLANES, LANES), F32),
            pltpu.VMEM((d // 2 // LANES, tb * SUBLANES, LANES), F32),
        ],
        compiler_params=pltpu.CompilerParams(
            dimension_semantics=("arbitrary",),
            vmem_limit_bytes=52 * 1024 * 1024),
        name="peer_v",
    )(ids_flat, act, base, tab, row(ln_g), row(ln_b))


def kernel(x, p, ln0_g, ln0_b, w_in, hg_lb, hg_norm_g, w_hg_branch, pool_w, pool_scale, w_out,
           ln1_g, ln1_b, w_query, sub_keys, u_tab, v_tab, w_ple_gate, w_ple_proj, ln2_g, ln2_b):
    batch, seq, d = x.shape
    depth = w_in.shape[0]
    t = batch * seq
    assert depth == 1 and seq % MIXER_TOKENS == 0 and t % ROUTER_TOKENS == 0 and t % PEER_TOKENS == 0
    alpha = (2.0 * depth) ** 0.25
    h1 = _mixer(x.reshape(t, d), batch, seq, alpha, ln0_g, ln0_b, w_in[0], hg_lb, hg_norm_g[0],
                w_hg_branch[0], pool_w[0], pool_scale[0], w_out[0], ln1_g[0], ln1_b[0])
    ids_t, gates_t, base = _router(h1, p[0].reshape(t, -1), alpha, w_query[0], sub_keys[0],
                                   w_ple_gate[0], w_ple_proj[0])
    ids_flat = ids_t.T.reshape(-1)
    act = _peer_u(ids_flat, h1, gates_t.T, _pack_table(u_tab[0]))
    out = _peer_v(ids_flat, act, base, _pack_table(v_tab[0]), ln2_g[0], ln2_b[0])
    return out.reshape(batch, seq, d)
```

```python
import functools

import jax
import jax.numpy as jnp
from jax import lax
from jax.experimental import pallas as pl
from jax.experimental.pallas import tpu as pltpu

F32 = jnp.float32
BF16 = jnp.bfloat16

HG_HEADS = 8
HG_DK = 128
HG_CHUNK = 32
POOL_WINDOWS = (2, 4, 8, 16)
POOL_GROUP = 128
MAX_WIN = 16
PEER_HEADS = 8
PEER_NKEYS = 128
PEER_HALF = 128
PEER_TOPK = 16
LN_EPS = 1e-5
RMS_EPS = 1e-6

LANES = 128
SUBLANES = 8

MIXER_TOKENS = 256
ROUTER_TOKENS = 256
PEER_TOKENS = 64
PACK_ROWS = 4


def _layer_norm(x, g, b):
    mu = jnp.mean(x, axis=-1, keepdims=True)
    xc = x - mu
    var = jnp.mean(xc * xc, axis=-1, keepdims=True)
    return xc * lax.rsqrt(var + LN_EPS) * g + b


def _sigmoid(x):
    return 1.0 / (1.0 + jnp.exp(-x))


def _dot_nt(a, b):
    return lax.dot_general(a, b, (((1,), (1,)), ((), ())), preferred_element_type=F32)


def _dot_tn(a, b):
    return lax.dot_general(a, b, (((0,), (0,)), ((), ())), preferred_element_type=F32)


def _mixer_kernel(alpha, x_ref, ln0g_ref, ln0b_ref, win_ref, hglb_ref, ng_ref, whg_ref, pw_ref,
                  ps_ref, wout_ref, ln1g_ref, ln1b_ref, h1_ref,
                  st_ref, ext_ref, qd_ref, ki_ref, ke_ref, vv_ref, dec_ref, o_ref, yb_ref, oi_ref):
    ts, d = x_ref.shape
    nc = ts // HG_CHUNK
    width = HG_HEADS * HG_DK
    pool_width = len(POOL_WINDOWS) * POOL_GROUP
    s_idx = pl.program_id(1)

    @pl.when(s_idx == 0)
    def _():
        st_ref[...] = jnp.zeros_like(st_ref)
        ext_ref[0:MAX_WIN, :] = jnp.zeros((MAX_WIN, pool_width), F32)

    h0 = _layer_norm(x_ref[...], ln0g_ref[...], ln0b_ref[...])
    hb = h0.astype(BF16)

    def proj(lo, hi):
        return jnp.dot(hb, win_ref[:, lo:hi], preferred_element_type=F32)

    lbl = hglb_ref[...]
    lmax = jnp.max(lbl, axis=0, keepdims=True)
    lexp = jnp.exp(lbl - lmax)
    lb = lexp[0:1, :] / jnp.sum(lexp, axis=0, keepdims=True)

    f = lb + (1.0 - lb) * _sigmoid(proj(width, 2 * width))
    kk = 1.0 - f
    bh = jnp.log(f)
    row_in_chunk = lax.broadcasted_iota(jnp.int32, (ts, 1), 0) % HG_CHUNK
    sh = 1
    while sh < HG_CHUNK:
        bh = bh + jnp.where(row_in_chunk >= sh, pltpu.roll(bh, sh, axis=0), 0.0)
        sh *= 2
    bh3 = bh.reshape(nc, HG_CHUNK, width)
    bl3 = bh3[:, HG_CHUNK - 1:HG_CHUNK, :]
    dec_ref[...] = jnp.exp(bl3.reshape(nc, width))
    ke_ref[...] = (kk * jnp.exp(jnp.broadcast_to(bl3, bh3.shape).reshape(ts, width) - bh)).astype(BF16)
    ki_ref[...] = (kk * jnp.exp(-bh)).astype(BF16)
    q = proj(0, width)
    qd_ref[...] = (q * _sigmoid(q) * (HG_DK ** -0.5) * jnp.exp(bh)).astype(BF16)
    vv_ref[...] = proj(2 * width, 3 * width).astype(BF16)
    g_out = proj(3 * width, 4 * width)
    g_act = g_out * _sigmoid(g_out)

    r_i = lax.broadcasted_iota(jnp.int32, (ts, ts), 0)
    c_i = lax.broadcasted_iota(jnp.int32, (ts, ts), 1)
    causal = (r_i // HG_CHUNK == c_i // HG_CHUNK) & (c_i <= r_i)

    for c in range(nc):
        rows = slice(c * HG_CHUNK, (c + 1) * HG_CHUNK)
        for h in range(HG_HEADS):
            cols = slice(h * HG_DK, (h + 1) * HG_DK)
            st = st_ref[h]
            oi_ref[rows, cols] = _dot_nt(qd_ref[rows, cols], st.astype(BF16))
            st_ref[h] = st * dec_ref[c:c + 1, cols] + _dot_tn(vv_ref[rows, cols], ke_ref[rows, cols])

    for h in range(HG_HEADS):
        cols = slice(h * HG_DK, (h + 1) * HG_DK)
        scores = jnp.where(causal, _dot_nt(qd_ref[:, cols], ki_ref[:, cols]), 0.0)
        o_h = jnp.dot(scores.astype(BF16), vv_ref[:, cols], preferred_element_type=F32) + oi_ref[:, cols]
        o_h = o_h * lax.rsqrt(jnp.mean(o_h * o_h, axis=-1, keepdims=True) + RMS_EPS) * ng_ref[...]
        o_ref[:, cols] = (o_h * g_act[:, cols]).astype(BF16)
    y_a = jnp.dot(o_ref[...], whg_ref[...], preferred_element_type=F32)

    v_pool = proj(4 * width, 4 * width + pool_width)
    ext_ref[MAX_WIN:MAX_WIN + ts, :] = v_pool
    pos = s_idx * ts + lax.broadcasted_iota(jnp.int32, (ts, 1), 0)
    out_group = d // len(POOL_WINDOWS)
    for gi, w in enumerate(POOL_WINDOWS):
        gcols = slice(gi * POOL_GROUP, (gi + 1) * POOL_GROUP)
        wsum = v_pool[:, gcols]
        for j in range(1, w):
            wsum = wsum + ext_ref[MAX_WIN - j:MAX_WIN - j + ts, gcols]
        cnt = jnp.minimum(pos + 1, w).astype(F32)
        pooled = wsum / cnt - v_pool[:, gcols]
        yb_ref[:, gi * out_group:(gi + 1) * out_group] = jnp.dot(
            pooled.astype(BF16), pw_ref[gi], preferred_element_type=F32)
    ext_ref[0:MAX_WIN, :] = ext_ref[ts:ts + MAX_WIN, :]
    y_b = yb_ref[...] * ps_ref[...]

    gate_a = proj(4 * width + pool_width, 4 * width + pool_width + d)
    gate_b = proj(4 * width + pool_width + d, 4 * width + pool_width + 2 * d)
    mix = _sigmoid(gate_a) * y_a + _sigmoid(gate_b) * y_b
    y = alpha * h0 + jnp.dot(mix.astype(BF16), wout_ref[...], preferred_element_type=F32)
    h1_ref[...] = _layer_norm(y, ln1g_ref[...], ln1b_ref[...])


def _const_spec(shape):
    nd = len(shape)
    return pl.BlockSpec(shape, lambda *_: (0,) * nd, pipeline_mode=pl.Buffered(1))


def _mixer(x2, batch, seq, alpha, ln0_g, ln0_b, w_in, hg_lb, norm_g, w_hg, pool_w, pool_scale,
           w_out, ln1_g, ln1_b):
    t, d = x2.shape
    ts = MIXER_TOKENS
    ns = seq // ts
    width = HG_HEADS * HG_DK
    pool_width = len(POOL_WINDOWS) * POOL_GROUP
    row = lambda a: a.reshape(1, -1)
    args = (x2, row(ln0_g), row(ln0_b), w_in.astype(BF16), hg_lb, row(norm_g), w_hg.astype(BF16),
            pool_w.astype(BF16), row(pool_scale), w_out.astype(BF16), row(ln1_g), row(ln1_b))
    in_specs = [pl.BlockSpec((ts, d), lambda b, s: (b * ns + s, 0))]
    in_specs += [_const_spec(a.shape) for a in args[1:]]
    return pl.pallas_call(
        functools.partial(_mixer_kernel, alpha),
        grid=(batch, ns),
        in_specs=in_specs,
        out_specs=pl.BlockSpec((ts, d), lambda b, s: (b * ns + s, 0)),
        out_shape=jax.ShapeDtypeStruct((t, d), F32),
        scratch_shapes=[
            pltpu.VMEM((HG_HEADS, HG_DK, HG_DK), F32),
            pltpu.VMEM((ts + MAX_WIN, pool_width), F32),
            pltpu.VMEM((ts, width), BF16),
            pltpu.VMEM((ts, width), BF16),
            pltpu.VMEM((ts, width), BF16),
            pltpu.VMEM((ts, width), BF16),
            pltpu.VMEM((ts // HG_CHUNK, width), F32),
            pltpu.VMEM((ts, width), BF16),
            pltpu.VMEM((ts, d), F32),
            pltpu.VMEM((ts, width), F32),
        ],
        compiler_params=pltpu.CompilerParams(
            dimension_semantics=("arbitrary", "arbitrary"),
            vmem_limit_bytes=48 * 1024 * 1024),
        name="mixer",
    )(*args)


def _extract_max(vals, pos, payloads):
    m = jnp.max(vals, axis=0, keepdims=True)
    sel = jnp.min(jnp.where(vals == m, pos, jnp.inf), axis=0, keepdims=True)
    hit = pos == sel
    picked = [jnp.max(jnp.where(hit, pay, -1.0), axis=0, keepdims=True) for pay in payloads]
    return m, sel, picked, jnp.where(hit, -jnp.inf, vals)


def _router_kernel(alpha, h_ref, p_ref, wq_ref, keys_ref, wpg_ref, wpp_ref, ids_ref, gates_ref, base_ref,
                   q_ref, ids_s, gates_s):
    h = h_ref[...]
    hb = h.astype(BF16)
    tb = h.shape[0]
    k = PEER_TOPK
    ple = _sigmoid(jnp.dot(hb, wpg_ref[...], preferred_element_type=F32)) * jnp.dot(
        p_ref[...].astype(BF16), wpp_ref[...], preferred_element_type=F32)
    base_ref[...] = alpha * h + ple

    q = jnp.dot(hb, wq_ref[...], preferred_element_type=F32).astype(BF16)
    for hp in range(2 * PEER_HEADS):
        q_ref[hp] = q[:, hp * PEER_HALF:(hp + 1) * PEER_HALF]

    key_iota = lax.broadcasted_iota(jnp.int32, (PEER_NKEYS, LANES), 0).astype(F32)
    row8_i = lax.broadcasted_iota(jnp.int32, (SUBLANES, LANES), 0)
    row8 = row8_i.astype(F32)
    row16 = lax.broadcasted_iota(jnp.int32, (k, LANES), 0).astype(F32)

    def head(hd, carry):
        for grp in range(tb // LANES):
            tok = slice(grp * LANES, (grp + 1) * LANES)
            tv, ti = [], []
            for half in range(2):
                sc = _dot_nt(keys_ref[hd, half], q_ref[2 * hd + half, tok, :])
                vals, idxs = [], []
                for _ in range(k):
                    m, sel, _, sc = _extract_max(sc, key_iota, [])
                    vals.append(m)
                    idxs.append(sel)
                tv.append(vals)
                ti.append(idxs)
            tv2 = jnp.concatenate(tv[1], axis=0)
            ti2 = jnp.concatenate(ti[1], axis=0)
            cands, poss, cids = [tv[0][0] + tv2], [row16], [ti[0][0] * PEER_NKEYS + ti2]
            for a in range(1, SUBLANES):
                keep = (row8_i + 1) * (a + 1) <= k
                cands.append(jnp.where(keep, tv[0][a] + tv2[0:SUBLANES], -jnp.inf))
                poss.append(row8 + a * k)
                cids.append(ti[0][a] * PEER_NKEYS + ti2[0:SUBLANES])
            cands.append(jnp.concatenate(tv[0][SUBLANES:], axis=0) + tv2[0:1])
            poss.append((row8 + SUBLANES) * k)
            cids.append(jnp.concatenate(ti[0][SUBLANES:], axis=0) * PEER_NKEYS + ti2[0:1])
            cand = jnp.concatenate(cands, axis=0)
            pos = jnp.concatenate(poss, axis=0)
            cid = jnp.concatenate(cids, axis=0)
            best_v, best_e = [], []
            for _ in range(k):
                m, _, (e,), cand = _extract_max(cand, pos, [cid])
                best_v.append(m)
                best_e.append(e)
            bv = jnp.concatenate(best_v, axis=0)
            ex = jnp.exp(bv - bv[0:1, :])
            rows = pl.ds(pl.multiple_of(hd * k, k), k)
            gates_s[rows, tok] = ex / jnp.sum(ex, axis=0, keepdims=True)
            ids_s[rows, tok] = jnp.concatenate(best_e, axis=0) * PACK_ROWS
        return carry

    lax.fori_loop(0, PEER_HEADS, head, 0)
    gates_ref[...] = gates_s[...].T
    ids_ref[...] = ids_s[...].T.astype(jnp.int32)


def _router(h1, p2, alpha, w_query, sub_keys, w_ple_gate, w_ple_proj):
    t, d = h1.shape
    tb = ROUTER_TOKENS
    slots = PEER_HEADS * PEER_TOPK
    args = (h1, p2, w_query.astype(BF16), sub_keys.astype(BF16), w_ple_gate.astype(BF16),
            w_ple_proj.astype(BF16))
    in_specs = [pl.BlockSpec((tb, d), lambda i: (i, 0)),
                pl.BlockSpec((tb, p2.shape[1]), lambda i: (i, 0))]
    in_specs += [_const_spec(a.shape) for a in args[2:]]
    return pl.pallas_call(
        functools.partial(_router_kernel, alpha),
        grid=(t // tb,),
        in_specs=in_specs,
        out_specs=[pl.BlockSpec((tb, slots), lambda i: (i, 0)),
                   pl.BlockSpec((tb, slots), lambda i: (i, 0)),
                   pl.BlockSpec((tb, d), lambda i: (i, 0))],
        out_shape=[jax.ShapeDtypeStruct((t, slots), jnp.int32),
                   jax.ShapeDtypeStruct((t, slots), F32),
                   jax.ShapeDtypeStruct((t, d), F32)],
        scratch_shapes=[pltpu.VMEM((2 * PEER_HEADS, tb, PEER_HALF), BF16),
                        pltpu.VMEM((slots, tb), F32),
                        pltpu.VMEM((slots, tb), F32)],
        compiler_params=pltpu.CompilerParams(
            dimension_semantics=("arbitrary",),
            vmem_limit_bytes=48 * 1024 * 1024),
        name="router",
    )(*args)


def _pack_table(tab):
    n, d = tab.shape
    halves = tab.astype(BF16).reshape(n, 2, d // 2)
    pairs = jnp.stack([halves[:, 0], halves[:, 1]], axis=-1)
    return lax.bitcast_convert_type(pairs, jnp.uint32).reshape(n * PACK_ROWS, LANES)


def _gather_rows(ids_ref, tab_ref, g_ref, t, slots):
    ids_t = ids_ref.at[pl.ds(t * slots, slots)]
    for k in range(slots):
        slab = tab_ref[pl.ds(pl.multiple_of(ids_t[k], PACK_ROWS), PACK_ROWS), :]
        start = (k // SUBLANES) * (PACK_ROWS * SUBLANES) + k % SUBLANES
        g_ref[pl.ds(start, PACK_ROWS, stride=SUBLANES), :] = slab


def _word_tile(g_ref, slots, c):
    return jnp.concatenate(
        [g_ref[(kt * PACK_ROWS + c) * SUBLANES:(kt * PACK_ROWS + c + 1) * SUBLANES, :]
         for kt in range(slots // SUBLANES)], axis=0)


def _gathered_matrix(g_ref, slots):
    words = jnp.concatenate([_word_tile(g_ref, slots, c) for c in range(PACK_ROWS)], axis=1)
    return pltpu.bitcast(words, BF16)


def _pipelined_tokens(tb, gather, compute, g0, g1):
    gather(0, g0)
    gather(1, g1)

    def pair(i, carry):
        t = 2 * i
        compute(t, g0)
        compute(t + 1, g1)
        gather(t + 2, g0)
        gather(t + 3, g1)
        return carry

    lax.fori_loop(0, tb // 2 - 1, pair, 0)
    compute(tb - 2, g0)
    compute(tb - 1, g1)


def _pair_select(slots):
    r = lax.broadcasted_iota(jnp.int32, (2 * slots, slots), 0)
    c = lax.broadcasted_iota(jnp.int32, (2 * slots, slots), 1)
    return (r // 2 == c).astype(F32)


def _gelu(x):
    return 0.5 * x * (1.0 + lax.erf(x * (2.0 ** -0.5)))


def _split_bf16(x):
    head = x.astype(BF16).astype(F32)
    return head, x - head


def _store_rows(ref, j, tb, value):
    for c in range(ref.shape[0]):
        ref[c, pl.ds(j, tb, stride=SUBLANES), :] = value[:, c * LANES:(c + 1) * LANES]


def _load_rows(ref, j, tb):
    return jnp.concatenate([ref[c, pl.ds(j, tb, stride=SUBLANES), :] for c in range(ref.shape[0])], axis=1)


def _load_group(ref, rows):
    return jnp.concatenate([ref[c, rows, :] for c in range(ref.shape[0])], axis=1)


def _store_group(ref, rows, value):
    for c in range(ref.shape[0]):
        ref[c, rows, :] = value[:, c * LANES:(c + 1) * LANES]


def _peer_u_kernel(ids_ref, h_ref, gate_ref, tab_ref, act_ref, g0_ref, g1_ref, lhs_ref, r_ref):
    tb, d = h_ref.shape
    slots = gate_ref.shape[1]
    half = d // 2
    parts = 4

    @pl.when(pl.program_id(0) == 0)
    def _():
        lhs_ref[...] = jnp.zeros_like(lhs_ref)

    x_hi, x_lo = _split_bf16(h_ref[...])
    for j, part in enumerate((x_hi[:, :half], x_lo[:, :half], x_hi[:, half:], x_lo[:, half:])):
        _store_rows(lhs_ref, j, tb, part)

    def gather(t, g_ref):
        _gather_rows(ids_ref, tab_ref, g_ref, t, slots)

    def compute(t, g_ref):
        rows = pl.ds(pl.multiple_of(t * SUBLANES, SUBLANES), SUBLANES)
        _store_group(r_ref, rows, _dot_nt(_load_group(lhs_ref, rows).astype(BF16), _gathered_matrix(g_ref, slots)))

    _pipelined_tokens(tb, gather, compute, g0_ref, g1_ref)

    r = [_load_rows(r_ref, j, tb) for j in range(parts)]
    lane_even = lax.broadcasted_iota(jnp.int32, (1, 2 * slots), 1) % 2 == 0
    inter = jnp.where(lane_even, r[0] + r[1], r[2] + r[3])
    pre = jnp.dot(inter, _pair_select(slots), preferred_element_type=F32, precision=lax.Precision.HIGHEST)
    act_ref[...] = _gelu(pre) * gate_ref[...]


def _peer_v_kernel(ids_ref, act_ref, base_ref, tab_ref, lng_ref, lnb_ref, out_ref, g0_ref, g1_ref, lhs_ref,
                   r_ref):
    tb, d = base_ref.shape
    slots = act_ref.shape[1]

    @pl.when(pl.program_id(0) == 0)
    def _():
        lhs_ref[...] = jnp.zeros_like(lhs_ref)

    a2 = lax.dot_general(act_ref[...], _pair_select(slots), (((1,), (1,)), ((), ())),
                         preferred_element_type=F32, precision=lax.Precision.HIGHEST)
    lane_even = lax.broadcasted_iota(jnp.int32, (1, 2 * slots), 1) % 2 == 0
    a_hi, a_lo = _split_bf16(a2)
    for j, part in enumerate((jnp.where(lane_even, a_hi, 0.0), jnp.where(lane_even, a_lo, 0.0),
                              jnp.where(lane_even, 0.0, a_hi), jnp.where(lane_even, 0.0, a_lo))):
        _store_rows(lhs_ref, j, tb, part)

    def gather(t, g_ref):
        _gather_rows(ids_ref, tab_ref, g_ref, t, slots)

    def compute(t, g_ref):
        rows = pl.ds(pl.multiple_of(t * SUBLANES, SUBLANES), SUBLANES)
        _store_group(r_ref, rows, jnp.dot(_load_group(lhs_ref, rows).astype(BF16),
                                          _gathered_matrix(g_ref, slots), preferred_element_type=F32))

    _pipelined_tokens(tb, gather, compute, g0_ref, g1_ref)

    r = [_load_rows(r_ref, j, tb) for j in range(4)]
    ffn = jnp.concatenate([r[0] + r[1], r[2] + r[3]], axis=1)
    out_ref[...] = _layer_norm(base_ref[...] + ffn, lng_ref[...], lnb_ref[...])


def _peer_specs(tb, d, slots, tab):
    ids_spec = pl.BlockSpec((tb * slots,), lambda i: (i,), memory_space=pltpu.SMEM)
    tile_spec = pl.BlockSpec((tb, d), lambda i: (i, 0))
    slot_spec = pl.BlockSpec((tb, slots), lambda i: (i, 0))
    return ids_spec, tile_spec, slot_spec, _const_spec(tab.shape)


def _peer_u(ids_flat, h1, gates, tab):
    t, d = h1.shape
    slots = gates.shape[1]
    tb = PEER_TOKENS
    ids_spec, tile_spec, slot_spec, tab_spec = _peer_specs(tb, d, slots, tab)
    return pl.pallas_call(
        _peer_u_kernel,
        grid=(t // tb,),
        in_specs=[ids_spec, tile_spec, slot_spec, tab_spec],
        out_specs=slot_spec,
        out_shape=jax.ShapeDtypeStruct((t, slots), F32),
        scratch_shapes=[
            pltpu.VMEM((slots * PACK_ROWS, LANES), jnp.uint32),
            pltpu.VMEM((slots * PACK_ROWS, LANES), jnp.uint32),
            pltpu.VMEM((d // 2 // LANES, tb * SUBLANES, LANES), F32),
            pltpu.VMEM((2 * slots // LANES, tb * SUBLANES, LANES), F32),
        ],
        compiler_params=pltpu.CompilerParams(
            dimension_semantics=("arbitrary",),
            vmem_limit_bytes=52 * 1024 * 1024),
        name="peer_u",
    )(ids_flat, h1, gates, tab)


def _peer_v(ids_flat, act, base, tab, ln_g, ln_b):
    t, d = base.shape
    slots = act.shape[1]
    tb = PEER_TOKENS
    ids_spec, tile_spec, slot_spec, tab_spec = _peer_specs(tb, d, slots, tab)
    row = lambda a: a.reshape(1, -1)
    return pl.pallas_call(
        _peer_v_kernel,
        grid=(t // tb,),
        in_specs=[ids_spec, slot_spec, tile_spec, tab_spec, _const_spec((1, d)), _const_spec((1, d))],
        out_specs=tile_spec,
        out_shape=jax.ShapeDtypeStruct((t, d), F32),
        scratch_shapes=[
            pltpu.VMEM((slots * PACK_ROWS, LANES), jnp.uint32),
            pltpu.VMEM((slots * PACK_ROWS, LANES), jnp.uint32),
            pltpu.VMEM((2 * slots // LANES, tb * SUBLANES, LANES), F32),
            pltpu.VMEM((d // 2 // LANES, tb * SUBLANES, LANES), F32),
        ],
        compiler_params=pltpu.CompilerParams(
            dimension_semantics=("arbitrary",),
            vmem_limit_bytes=52 * 1024 * 1024),
        name="peer_v",
    )(ids_flat, act, base, tab, row(ln_g), row(ln_b))


def kernel(x, p, ln0_g, ln0_b, w_in, hg_lb, hg_norm_g, w_hg_branch, pool_w, pool_scale, w_out,
           ln1_g, ln1_b, w_query, sub_keys, u_tab, v_tab, w_ple_gate, w_ple_proj, ln2_g, ln2_b):
    batch, seq, d = x.shape
    depth = w_in.shape[0]
    t = batch * seq
    assert depth == 1 and seq % MIXER_TOKENS == 0 and t % ROUTER_TOKENS == 0 and t % PEER_TOKENS == 0
    alpha = (2.0 * depth) ** 0.25
    h1 = _mixer(x.reshape(t, d), batch, seq, alpha, ln0_g, ln0_b, w_in[0], hg_lb, hg_norm_g[0],
                w_hg_branch[0], pool_w[0], pool_scale[0], w_out[0], ln1_g[0], ln1_b[0])
    ids, gates, base = _router(h1, p[0].reshape(t, -1), alpha, w_query[0], sub_keys[0],
                               w_ple_gate[0], w_ple_proj[0])
    ids_flat = ids.reshape(-1)
    act = _peer_u(ids_flat, h1, gates, _pack_table(u_tab[0]))
    out = _peer_v(ids_flat, act, base, _pack_table(v_tab[0]), ln2_g[0], ln2_b[0])
    return out.reshape(batch, seq, d)
```

```python
import functools

import jax
import jax.numpy as jnp
from jax import lax
from jax.experimental import pallas as pl
from jax.experimental.pallas import tpu as pltpu

F32 = jnp.float32
BF16 = jnp.bfloat16

HG_HEADS = 8
HG_DK = 128
HG_CHUNK = 32
POOL_WINDOWS = (2, 4, 8, 16)
POOL_GROUP = 128
MAX_WIN = 16
PEER_HEADS = 8
PEER_NKEYS = 128
PEER_HALF = 128
PEER_TOPK = 16
LN_EPS = 1e-5
RMS_EPS = 1e-6

LANES = 128
SUBLANES = 8

MIXER_TOKENS = 256
ROUTER_TOKENS = 1024
PEER_TOKENS = 128
PACK_ROWS = 4


def _layer_norm(x, g, b):
    mu = jnp.mean(x, axis=-1, keepdims=True)
    xc = x - mu
    var = jnp.mean(xc * xc, axis=-1, keepdims=True)
    return xc * lax.rsqrt(var + LN_EPS) * g + b


def _sigmoid(x):
    return 1.0 / (1.0 + jnp.exp(-x))


def _dot_nt(a, b):
    return lax.dot_general(a, b, (((1,), (1,)), ((), ())), preferred_element_type=F32)


def _dot_tn(a, b):
    return lax.dot_general(a, b, (((0,), (0,)), ((), ())), preferred_element_type=F32)


def _mixer_kernel(alpha, x_ref, ln0g_ref, ln0b_ref, win_ref, hglb_ref, ng_ref, whg_ref, pw_ref,
                  ps_ref, wout_ref, ln1g_ref, ln1b_ref, h1_ref,
                  st_ref, ext_ref, qd_ref, ki_ref, ke_ref, vv_ref, dec_ref, o_ref, yb_ref, oi_ref):
    ts, d = x_ref.shape
    nc = ts // HG_CHUNK
    width = HG_HEADS * HG_DK
    pool_width = len(POOL_WINDOWS) * POOL_GROUP
    s_idx = pl.program_id(1)

    @pl.when(s_idx == 0)
    def _():
        st_ref[...] = jnp.zeros_like(st_ref)
        ext_ref[0:MAX_WIN, :] = jnp.zeros((MAX_WIN, pool_width), F32)

    h0 = _layer_norm(x_ref[...], ln0g_ref[...], ln0b_ref[...])
    hb = h0.astype(BF16)

    def proj(lo, hi):
        return jnp.dot(hb, win_ref[:, lo:hi], preferred_element_type=F32)

    lbl = hglb_ref[...]
    lmax = jnp.max(lbl, axis=0, keepdims=True)
    lexp = jnp.exp(lbl - lmax)
    lb = lexp[0:1, :] / jnp.sum(lexp, axis=0, keepdims=True)

    f = lb + (1.0 - lb) * _sigmoid(proj(width, 2 * width))
    kk = 1.0 - f
    bh = jnp.log(f)
    row_in_chunk = lax.broadcasted_iota(jnp.int32, (ts, 1), 0) % HG_CHUNK
    sh = 1
    while sh < HG_CHUNK:
        bh = bh + jnp.where(row_in_chunk >= sh, pltpu.roll(bh, sh, axis=0), 0.0)
        sh *= 2
    bh3 = bh.reshape(nc, HG_CHUNK, width)
    bl3 = bh3[:, HG_CHUNK - 1:HG_CHUNK, :]
    dec_ref[...] = jnp.exp(bl3.reshape(nc, width))
    ke_ref[...] = (kk * jnp.exp(jnp.broadcast_to(bl3, bh3.shape).reshape(ts, width) - bh)).astype(BF16)
    ki_ref[...] = (kk * jnp.exp(-bh)).astype(BF16)
    q = proj(0, width)
    qd_ref[...] = (q * _sigmoid(q) * (HG_DK ** -0.5) * jnp.exp(bh)).astype(BF16)
    vv_ref[...] = proj(2 * width, 3 * width).astype(BF16)
    g_out = proj(3 * width, 4 * width)
    g_act = g_out * _sigmoid(g_out)

    r_i = lax.broadcasted_iota(jnp.int32, (ts, ts), 0)
    c_i = lax.broadcasted_iota(jnp.int32, (ts, ts), 1)
    causal = (r_i // HG_CHUNK == c_i // HG_CHUNK) & (c_i <= r_i)

    for c in range(nc):
        rows = slice(c * HG_CHUNK, (c + 1) * HG_CHUNK)
        for h in range(HG_HEADS):
            cols = slice(h * HG_DK, (h + 1) * HG_DK)
            st = st_ref[h]
            oi_ref[rows, cols] = _dot_nt(qd_ref[rows, cols], st.astype(BF16))
            st_ref[h] = st * dec_ref[c:c + 1, cols] + _dot_tn(vv_ref[rows, cols], ke_ref[rows, cols])

    for h in range(HG_HEADS):
        cols = slice(h * HG_DK, (h + 1) * HG_DK)
        scores = jnp.where(causal, _dot_nt(qd_ref[:, cols], ki_ref[:, cols]), 0.0)
        o_h = jnp.dot(scores.astype(BF16), vv_ref[:, cols], preferred_element_type=F32) + oi_ref[:, cols]
        o_h = o_h * lax.rsqrt(jnp.mean(o_h * o_h, axis=-1, keepdims=True) + RMS_EPS) * ng_ref[...]
        o_ref[:, cols] = (o_h * g_act[:, cols]).astype(BF16)
    y_a = jnp.dot(o_ref[...], whg_ref[...], preferred_element_type=F32)

    v_pool = proj(4 * width, 4 * width + pool_width)
    ext_ref[MAX_WIN:MAX_WIN + ts, :] = v_pool
    pos = s_idx * ts + lax.broadcasted_iota(jnp.int32, (ts, 1), 0)
    out_group = d // len(POOL_WINDOWS)
    for gi, w in enumerate(POOL_WINDOWS):
        gcols = slice(gi * POOL_GROUP, (gi + 1) * POOL_GROUP)
        wsum = v_pool[:, gcols]
        for j in range(1, w):
            wsum = wsum + ext_ref[MAX_WIN - j:MAX_WIN - j + ts, gcols]
        cnt = jnp.minimum(pos + 1, w).astype(F32)
        pooled = wsum / cnt - v_pool[:, gcols]
        yb_ref[:, gi * out_group:(gi + 1) * out_group] = jnp.dot(
            pooled.astype(BF16), pw_ref[gi], preferred_element_type=F32)
    ext_ref[0:MAX_WIN, :] = ext_ref[ts:ts + MAX_WIN, :]
    y_b = yb_ref[...] * ps_ref[...]

    gate_a = proj(4 * width + pool_width, 4 * width + pool_width + d)
    gate_b = proj(4 * width + pool_width + d, 4 * width + pool_width + 2 * d)
    mix = _sigmoid(gate_a) * y_a + _sigmoid(gate_b) * y_b
    y = alpha * h0 + jnp.dot(mix.astype(BF16), wout_ref[...], preferred_element_type=F32)
    h1_ref[...] = _layer_norm(y, ln1g_ref[...], ln1b_ref[...])


def _const_spec(shape):
    nd = len(shape)
    return pl.BlockSpec(shape, lambda *_: (0,) * nd, pipeline_mode=pl.Buffered(1))


def _mixer(x2, batch, seq, alpha, ln0_g, ln0_b, w_in, hg_lb, norm_g, w_hg, pool_w, pool_scale,
           w_out, ln1_g, ln1_b):
    t, d = x2.shape
    ts = MIXER_TOKENS
    ns = seq // ts
    width = HG_HEADS * HG_DK
    pool_width = len(POOL_WINDOWS) * POOL_GROUP
    row = lambda a: a.reshape(1, -1)
    args = (x2, row(ln0_g), row(ln0_b), w_in.astype(BF16), hg_lb, row(norm_g), w_hg.astype(BF16),
            pool_w.astype(BF16), row(pool_scale), w_out.astype(BF16), row(ln1_g), row(ln1_b))
    in_specs = [pl.BlockSpec((ts, d), lambda b, s: (b * ns + s, 0))]
    in_specs += [_const_spec(a.shape) for a in args[1:]]
    return pl.pallas_call(
        functools.partial(_mixer_kernel, alpha),
        grid=(batch, ns),
        in_specs=in_specs,
        out_specs=pl.BlockSpec((ts, d), lambda b, s: (b * ns + s, 0)),
        out_shape=jax.ShapeDtypeStruct((t, d), F32),
        scratch_shapes=[
            pltpu.VMEM((HG_HEADS, HG_DK, HG_DK), F32),
            pltpu.VMEM((ts + MAX_WIN, pool_width), F32),
            pltpu.VMEM((ts, width), BF16),
            pltpu.VMEM((ts, width), BF16),
            pltpu.VMEM((ts, width), BF16),
            pltpu.VMEM((ts, width), BF16),
            pltpu.VMEM((ts // HG_CHUNK, width), F32),
            pltpu.VMEM((ts, width), BF16),
            pltpu.VMEM((ts, d), F32),
            pltpu.VMEM((ts, width), F32),
        ],
        compiler_params=pltpu.CompilerParams(
            dimension_semantics=("arbitrary", "arbitrary"),
            vmem_limit_bytes=48 * 1024 * 1024),
        name="mixer",
    )(*args)


def _sort_network(n):
    def merge(lo, hi, r):
        step = r * 2
        if step < hi - lo:
            yield from merge(lo, hi, step)
            yield from merge(lo + r, hi, step)
            yield from [(i, i + r) for i in range(lo + r, hi - r, step)]
        else:
            yield (lo, lo + r)

    def sort(lo, hi):
        if hi - lo >= 1:
            mid = lo + (hi - lo) // 2
            yield from sort(lo, mid)
            yield from sort(mid + 1, hi)
            yield from merge(lo, hi, 1)

    return tuple(sort(0, n - 1))


def _bitonic_merge_network(n):
    pairs, dist = [], n // 2
    while dist:
        pairs += [(i, i + dist) for i in range(n) if not i & dist]
        dist //= 2
    return tuple(pairs)


def _ordered(a, b):
    a_first = (a[0] > b[0]) | ((a[0] == b[0]) & (a[1] < b[1]))
    first = (jnp.maximum(a[0], b[0]),) + tuple(jnp.where(a_first, x, y) for x, y in zip(a[1:], b[1:]))
    second = (jnp.minimum(a[0], b[0]),) + tuple(jnp.where(a_first, y, x) for x, y in zip(a[1:], b[1:]))
    return first, second


def _top_k_sorted(elems, k):
    blocks = []
    for s in range(0, len(elems), k):
        blk = list(elems[s:s + k])
        for i, j in _sort_network(k):
            blk[i], blk[j] = _ordered(blk[i], blk[j])
        blocks.append(blk)
    while len(blocks) > 1:
        merged = []
        for a, b in zip(blocks[0::2], blocks[1::2]):
            c = [_ordered(a[i], b[k - 1 - i])[0] for i in range(k)]
            for i, j in _bitonic_merge_network(k):
                c[i], c[j] = _ordered(c[i], c[j])
            merged.append(c)
        blocks = merged
    return blocks[0]


def _router_kernel(alpha, h_ref, p_ref, wq_ref, keys_ref, wpg_ref, wpp_ref, ids_ref, gates_ref, base_ref,
                   q_ref, sc_ref, ids_s, gates_s):
    h = h_ref[...]
    hb = h.astype(BF16)
    tb = h.shape[0]
    k = PEER_TOPK
    planes = tb // LANES
    vreg = (planes, LANES)
    ple = _sigmoid(jnp.dot(hb, wpg_ref[...], preferred_element_type=F32)) * jnp.dot(
        p_ref[...].astype(BF16), wpp_ref[...], preferred_element_type=F32)
    base_ref[...] = alpha * h + ple

    for hd in range(PEER_HEADS):
        cols = slice(hd * 2 * PEER_HALF, (hd + 1) * 2 * PEER_HALF)
        q = jnp.dot(hb, wq_ref[:, cols], preferred_element_type=F32).astype(BF16)
        q_ref[2 * hd] = q[:, :PEER_HALF]
        q_ref[2 * hd + 1] = q[:, PEER_HALF:]

    def per_key_rows(ref, row):
        return ref.at[pl.ds(row, planes, stride=LANES), :]

    def head(hd, carry):
        tops = []
        for half in range(2):
            sc = _dot_nt(keys_ref[hd, half], q_ref[2 * hd + half])
            for pln in range(planes):
                sc_ref[pln * LANES:(pln + 1) * LANES, :] = sc[:, pln * LANES:(pln + 1) * LANES]
            elems = [(per_key_rows(sc_ref, key)[...], jnp.full(vreg, float(key), F32))
                     for key in range(PEER_NKEYS)]
            tops.append(_top_k_sorted(elems, k))
        (top1, top2) = tops
        cands = [(top1[a][0] + top2[b][0], jnp.full(vreg, float(a * k + b), F32),
                  top1[a][1] * PEER_NKEYS + top2[b][1])
                 for a in range(k) for b in range(k) if (a + 1) * (b + 1) <= k]
        while len(cands) % k or len(cands) // k & (len(cands) // k - 1):
            cands.append((jnp.full(vreg, -jnp.inf, F32), jnp.full(vreg, float(k * k + len(cands)), F32),
                          jnp.zeros(vreg, F32)))
        best = _top_k_sorted(cands, k)
        ex = [jnp.exp(e[0] - best[0][0]) for e in best]
        denom = functools.reduce(jnp.add, ex)
        for j in range(k):
            slot = hd * k + j
            per_key_rows(gates_s, slot)[...] = ex[j] / denom
            per_key_rows(ids_s, slot)[...] = best[j][2] * PACK_ROWS
        return carry

    lax.fori_loop(0, PEER_HEADS, head, 0)
    for pln in range(planes):
        rows = slice(pln * LANES, (pln + 1) * LANES)
        gates_ref[rows, :] = gates_s[rows, :].T
        ids_ref[rows, :] = ids_s[rows, :].T.astype(jnp.int32)


def _router(h1, p2, alpha, w_query, sub_keys, w_ple_gate, w_ple_proj):
    t, d = h1.shape
    tb = ROUTER_TOKENS
    slots = PEER_HEADS * PEER_TOPK
    assert slots == LANES and PEER_NKEYS == LANES
    args = (h1, p2, w_query.astype(BF16), sub_keys.astype(BF16), w_ple_gate.astype(BF16),
            w_ple_proj.astype(BF16))
    in_specs = [pl.BlockSpec((tb, d), lambda i: (i, 0)),
                pl.BlockSpec((tb, p2.shape[1]), lambda i: (i, 0))]
    in_specs += [_const_spec(a.shape) for a in args[2:]]
    return pl.pallas_call(
        functools.partial(_router_kernel, alpha),
        grid=(t // tb,),
        in_specs=in_specs,
        out_specs=[pl.BlockSpec((tb, slots), lambda i: (i, 0)),
                   pl.BlockSpec((tb, slots), lambda i: (i, 0)),
                   pl.BlockSpec((tb, d), lambda i: (i, 0))],
        out_shape=[jax.ShapeDtypeStruct((t, slots), jnp.int32),
                   jax.ShapeDtypeStruct((t, slots), F32),
                   jax.ShapeDtypeStruct((t, d), F32)],
        scratch_shapes=[pltpu.VMEM((2 * PEER_HEADS, tb, PEER_HALF), BF16),
                        pltpu.VMEM((tb, LANES), F32),
                        pltpu.VMEM((tb, LANES), F32),
                        pltpu.VMEM((tb, LANES), F32)],
        compiler_params=pltpu.CompilerParams(
            dimension_semantics=("arbitrary",),
            vmem_limit_bytes=56 * 1024 * 1024),
        name="router",
    )(*args)


def _pack_table(tab):
    n, d = tab.shape
    halves = tab.astype(BF16).reshape(n, 2, d // 2)
    pairs = jnp.stack([halves[:, 0], halves[:, 1]], axis=-1)
    return lax.bitcast_convert_type(pairs, jnp.uint32).reshape(n * PACK_ROWS, LANES)


def _gather_rows(ids_ref, tab_ref, g_ref, t, slots):
    ids_t = ids_ref.at[pl.ds(t * slots, slots)]
    for k in range(slots):
        slab = tab_ref[pl.ds(pl.multiple_of(ids_t[k], PACK_ROWS), PACK_ROWS), :]
        start = (k // SUBLANES) * (PACK_ROWS * SUBLANES) + k % SUBLANES
        g_ref[pl.ds(start, PACK_ROWS, stride=SUBLANES), :] = slab


def _word_tile(g_ref, slots, c):
    return jnp.concatenate(
        [g_ref[(kt * PACK_ROWS + c) * SUBLANES:(kt * PACK_ROWS + c + 1) * SUBLANES, :]
         for kt in range(slots // SUBLANES)], axis=0)


def _gathered_matrix(g_ref, slots):
    words = jnp.concatenate([_word_tile(g_ref, slots, c) for c in range(PACK_ROWS)], axis=1)
    return pltpu.bitcast(words, BF16)


def _pipelined_tokens(tb, gather, compute, bufs):
    g0, g1 = bufs
    gather(0, g0)
    gather(1, g1)

    def pair(i, carry):
        t = 2 * i
        compute(t, g0)
        gather(t + 2, g0)
        compute(t + 1, g1)
        gather(t + 3, g1)
        return carry

    lax.fori_loop(0, tb // 2 - 1, pair, 0)
    compute(tb - 2, g0)
    compute(tb - 1, g1)


def _pair_select(slots):
    r = lax.broadcasted_iota(jnp.int32, (2 * slots, slots), 0)
    c = lax.broadcasted_iota(jnp.int32, (2 * slots, slots), 1)
    return (r // 2 == c).astype(F32)


def _gelu(x):
    return 0.5 * x * (1.0 + lax.erf(x * (2.0 ** -0.5)))


def _split_bf16(x):
    head = x.astype(BF16).astype(F32)
    return head, x - head


def _store_rows(ref, j, tb, value):
    for c in range(ref.shape[0]):
        ref[c, pl.ds(j, tb, stride=SUBLANES), :] = value[:, c * LANES:(c + 1) * LANES]


def _load_rows(ref, j, tb):
    return jnp.concatenate([ref[c, pl.ds(j, tb, stride=SUBLANES), :] for c in range(ref.shape[0])], axis=1)


def _load_group(ref, rows):
    return jnp.concatenate([ref[c, rows, :] for c in range(ref.shape[0])], axis=1)


def _store_group(ref, rows, value):
    for c in range(ref.shape[0]):
        ref[c, rows, :] = value[:, c * LANES:(c + 1) * LANES]


def _peer_u_kernel(ids_ref, h_ref, gate_ref, tab_ref, act_ref, g0_ref, g1_ref, lhs_ref, r_ref):
    tb, d = h_ref.shape
    slots = gate_ref.shape[1]
    half = d // 2
    parts = 4

    @pl.when(pl.program_id(0) == 0)
    def _():
        lhs_ref[...] = jnp.zeros_like(lhs_ref)

    x_hi, x_lo = _split_bf16(h_ref[...])
    for j, part in enumerate((x_hi[:, :half], x_lo[:, :half], x_hi[:, half:], x_lo[:, half:])):
        _store_rows(lhs_ref, j, tb, part)

    def gather(t, g_ref):
        _gather_rows(ids_ref, tab_ref, g_ref, t, slots)

    def compute(t, g_ref):
        rows = pl.ds(pl.multiple_of(t * SUBLANES, SUBLANES), SUBLANES)
        _store_group(r_ref, rows, _dot_nt(_load_group(lhs_ref, rows).astype(BF16), _gathered_matrix(g_ref, slots)))

    _pipelined_tokens(tb, gather, compute, (g0_ref, g1_ref))

    r = [_load_rows(r_ref, j, tb) for j in range(parts)]
    lane_even = lax.broadcasted_iota(jnp.int32, (1, 2 * slots), 1) % 2 == 0
    inter = jnp.where(lane_even, r[0] + r[1], r[2] + r[3])
    pre = jnp.dot(inter, _pair_select(slots), preferred_element_type=F32, precision=lax.Precision.HIGHEST)
    act_ref[...] = _gelu(pre) * gate_ref[...]


def _peer_v_kernel(ids_ref, act_ref, base_ref, tab_ref, lng_ref, lnb_ref, out_ref, g0_ref, g1_ref, lhs_ref,
                   r_ref):
    tb, d = base_ref.shape
    slots = act_ref.shape[1]

    @pl.when(pl.program_id(0) == 0)
    def _():
        lhs_ref[...] = jnp.zeros_like(lhs_ref)

    a2 = lax.dot_general(act_ref[...], _pair_select(slots), (((1,), (1,)), ((), ())),
                         preferred_element_type=F32, precision=lax.Precision.HIGHEST)
    lane_even = lax.broadcasted_iota(jnp.int32, (1, 2 * slots), 1) % 2 == 0
    a_hi, a_lo = _split_bf16(a2)
    for j, part in enumerate((jnp.where(lane_even, a_hi, 0.0), jnp.where(lane_even, a_lo, 0.0),
                              jnp.where(lane_even, 0.0, a_hi), jnp.where(lane_even, 0.0, a_lo))):
        _store_rows(lhs_ref, j, tb, part)

    def gather(t, g_ref):
        _gather_rows(ids_ref, tab_ref, g_ref, t, slots)

    def compute(t, g_ref):
        rows = pl.ds(pl.multiple_of(t * SUBLANES, SUBLANES), SUBLANES)
        _store_group(r_ref, rows, jnp.dot(_load_group(lhs_ref, rows).astype(BF16),
                                          _gathered_matrix(g_ref, slots), preferred_element_type=F32))

    _pipelined_tokens(tb, gather, compute, (g0_ref, g1_ref))

    r = [_load_rows(r_ref, j, tb) for j in range(4)]
    ffn = jnp.concatenate([r[0] + r[1], r[2] + r[3]], axis=1)
    out_ref[...] = _layer_norm(base_ref[...] + ffn, lng_ref[...], lnb_ref[...])


def _peer_specs(tb, d, slots, tab):
    ids_spec = pl.BlockSpec((tb * slots,), lambda i: (i,), memory_space=pltpu.SMEM)
    tile_spec = pl.BlockSpec((tb, d), lambda i: (i, 0))
    slot_spec = pl.BlockSpec((tb, slots), lambda i: (i, 0))
    return ids_spec, tile_spec, slot_spec, _const_spec(tab.shape)


def _peer_u(ids_flat, h1, gates, tab):
    t, d = h1.shape
    slots = gates.shape[1]
    tb = PEER_TOKENS
    ids_spec, tile_spec, slot_spec, tab_spec = _peer_specs(tb, d, slots, tab)
    return pl.pallas_call(
        _peer_u_kernel,
        grid=(t // tb,),
        in_specs=[ids_spec, tile_spec, slot_spec, tab_spec],
        out_specs=slot_spec,
        out_shape=jax.ShapeDtypeStruct((t, slots), F32),
        scratch_shapes=[
            *[pltpu.VMEM((slots * PACK_ROWS, LANES), jnp.uint32)] * 2,
            pltpu.VMEM((d // 2 // LANES, tb * SUBLANES, LANES), F32),
            pltpu.VMEM((2 * slots // LANES, tb * SUBLANES, LANES), F32),
        ],
        compiler_params=pltpu.CompilerParams(
            dimension_semantics=("arbitrary",),
            vmem_limit_bytes=52 * 1024 * 1024),
        name="peer_u",
    )(ids_flat, h1, gates, tab)


def _peer_v(ids_flat, act, base, tab, ln_g, ln_b):
    t, d = base.shape
    slots = act.shape[1]
    tb = PEER_TOKENS
    ids_spec, tile_spec, slot_spec, tab_spec = _peer_specs(tb, d, slots, tab)
    row = lambda a: a.reshape(1, -1)
    return pl.pallas_call(
        _peer_v_kernel,
        grid=(t // tb,),
        in_specs=[ids_spec, slot_spec, tile_spec, tab_spec, _const_spec((1, d)), _const_spec((1, d))],
        out_specs=tile_spec,
        out_shape=jax.ShapeDtypeStruct((t, d), F32),
        scratch_shapes=[
            *[pltpu.VMEM((slots * PACK_ROWS, LANES), jnp.uint32)] * 2,
            pltpu.VMEM((2 * slots // LANES, tb * SUBLANES, LANES), F32),
            pltpu.VMEM((d // 2 // LANES, tb * SUBLANES, LANES), F32),
        ],
        compiler_params=pltpu.CompilerParams(
            dimension_semantics=("arbitrary",),
            vmem_limit_bytes=52 * 1024 * 1024),
        name="peer_v",
    )(ids_flat, act, base, tab, row(ln_g), row(ln_b))


def kernel(x, p, ln0_g, ln0_b, w_in, hg_lb, hg_norm_g, w_hg_branch, pool_w, pool_scale, w_out,
           ln1_g, ln1_b, w_query, sub_keys, u_tab, v_tab, w_ple_gate, w_ple_proj, ln2_g, ln2_b):
    batch, seq, d = x.shape
    depth = w_in.shape[0]
    t = batch * seq
    assert depth == 1 and seq % MIXER_TOKENS == 0 and t % ROUTER_TOKENS == 0 and t % PEER_TOKENS == 0
    alpha = (2.0 * depth) ** 0.25
    h1 = _mixer(x.reshape(t, d), batch, seq, alpha, ln0_g, ln0_b, w_in[0], hg_lb, hg_norm_g[0],
                w_hg_branch[0], pool_w[0], pool_scale[0], w_out[0], ln1_g[0], ln1_b[0])
    ids, gates, base = _router(h1, p[0].reshape(t, -1), alpha, w_query[0], sub_keys[0],
                               w_ple_gate[0], w_ple_proj[0])
    ids_flat = ids.reshape(-1)
    act = _peer_u(ids_flat, h1, gates, _pack_table(u_tab[0]))
    out = _peer_v(ids_flat, act, base, _pack_table(v_tab[0]), ln2_g[0], ln2_b[0])
    return out.reshape(batch, seq, d)
```

```python
import functools

import jax
import jax.numpy as jnp
from jax import lax
from jax.experimental import pallas as pl
from jax.experimental.pallas import tpu as pltpu

F32 = jnp.float32
BF16 = jnp.bfloat16

HG_HEADS = 8
HG_DK = 128
HG_CHUNK = 32
POOL_WINDOWS = (2, 4, 8, 16)
POOL_GROUP = 128
MAX_WIN = 16
PEER_HEADS = 8
PEER_NKEYS = 128
PEER_HALF = 128
PEER_TOPK = 16
LN_EPS = 1e-5
RMS_EPS = 1e-6

LANES = 128
SUBLANES = 8

MIXER_TOKENS = 256
ROUTER_TOKENS = 1024
PEER_TOKENS = 256
PACK_ROWS = 4
PACK_BLOCK_ROWS = 512


def _layer_norm(x, g, b):
    mu = jnp.mean(x, axis=-1, keepdims=True)
    xc = x - mu
    var = jnp.mean(xc * xc, axis=-1, keepdims=True)
    return xc * lax.rsqrt(var + LN_EPS) * g + b


def _sigmoid(x):
    return 1.0 / (1.0 + jnp.exp(-x))


def _dot_nt(a, b):
    return lax.dot_general(a, b, (((1,), (1,)), ((), ())), preferred_element_type=F32)


def _dot_tn(a, b):
    return lax.dot_general(a, b, (((0,), (0,)), ((), ())), preferred_element_type=F32)


def _mixer_kernel(alpha, x_ref, ln0g_ref, ln0b_ref, win_ref, hglb_ref, ng_ref, whg_ref, pw_ref,
                  ps_ref, wout_ref, ln1g_ref, ln1b_ref, h1_ref,
                  st_ref, ext_ref, qd_ref, ki_ref, ke_ref, vv_ref, dec_ref, o_ref, yb_ref, oi_ref):
    ts, d = x_ref.shape
    nc = ts // HG_CHUNK
    width = HG_HEADS * HG_DK
    pool_width = len(POOL_WINDOWS) * POOL_GROUP
    s_idx = pl.program_id(1)

    @pl.when(s_idx == 0)
    def _():
        st_ref[...] = jnp.zeros_like(st_ref)
        ext_ref[0:MAX_WIN, :] = jnp.zeros((MAX_WIN, pool_width), F32)

    h0 = _layer_norm(x_ref[...], ln0g_ref[...], ln0b_ref[...])
    hb = h0.astype(BF16)

    def proj(lo, hi):
        return jnp.dot(hb, win_ref[:, lo:hi], preferred_element_type=F32)

    lbl = hglb_ref[...]
    lmax = jnp.max(lbl, axis=0, keepdims=True)
    lexp = jnp.exp(lbl - lmax)
    lb = lexp[0:1, :] / jnp.sum(lexp, axis=0, keepdims=True)

    f = lb + (1.0 - lb) * _sigmoid(proj(width, 2 * width))
    kk = 1.0 - f
    bh = jnp.log(f)
    row_in_chunk = lax.broadcasted_iota(jnp.int32, (ts, 1), 0) % HG_CHUNK
    sh = 1
    while sh < HG_CHUNK:
        bh = bh + jnp.where(row_in_chunk >= sh, pltpu.roll(bh, sh, axis=0), 0.0)
        sh *= 2
    bh3 = bh.reshape(nc, HG_CHUNK, width)
    bl3 = bh3[:, HG_CHUNK - 1:HG_CHUNK, :]
    dec_ref[...] = jnp.exp(bl3.reshape(nc, width))
    ke_ref[...] = (kk * jnp.exp(jnp.broadcast_to(bl3, bh3.shape).reshape(ts, width) - bh)).astype(BF16)
    ki_ref[...] = (kk * jnp.exp(-bh)).astype(BF16)
    q = proj(0, width)
    qd_ref[...] = (q * _sigmoid(q) * (HG_DK ** -0.5) * jnp.exp(bh)).astype(BF16)
    vv_ref[...] = proj(2 * width, 3 * width).astype(BF16)
    g_out = proj(3 * width, 4 * width)
    g_act = g_out * _sigmoid(g_out)

    r_i = lax.broadcasted_iota(jnp.int32, (ts, ts), 0)
    c_i = lax.broadcasted_iota(jnp.int32, (ts, ts), 1)
    causal = (r_i // HG_CHUNK == c_i // HG_CHUNK) & (c_i <= r_i)

    for c in range(nc):
        rows = slice(c * HG_CHUNK, (c + 1) * HG_CHUNK)
        for h in range(HG_HEADS):
            cols = slice(h * HG_DK, (h + 1) * HG_DK)
            st = st_ref[h]
            oi_ref[rows, cols] = _dot_nt(qd_ref[rows, cols], st.astype(BF16))
            st_ref[h] = st * dec_ref[c:c + 1, cols] + _dot_tn(vv_ref[rows, cols], ke_ref[rows, cols])

    for h in range(HG_HEADS):
        cols = slice(h * HG_DK, (h + 1) * HG_DK)
        scores = jnp.where(causal, _dot_nt(qd_ref[:, cols], ki_ref[:, cols]), 0.0)
        o_h = jnp.dot(scores.astype(BF16), vv_ref[:, cols], preferred_element_type=F32) + oi_ref[:, cols]
        o_h = o_h * lax.rsqrt(jnp.mean(o_h * o_h, axis=-1, keepdims=True) + RMS_EPS) * ng_ref[...]
        o_ref[:, cols] = (o_h * g_act[:, cols]).astype(BF16)
    y_a = jnp.dot(o_ref[...], whg_ref[...], preferred_element_type=F32)

    v_pool = proj(4 * width, 4 * width + pool_width)
    ext_ref[MAX_WIN:MAX_WIN + ts, :] = v_pool
    pos = s_idx * ts + lax.broadcasted_iota(jnp.int32, (ts, 1), 0)
    out_group = d // len(POOL_WINDOWS)
    for gi, w in enumerate(POOL_WINDOWS):
        gcols = slice(gi * POOL_GROUP, (gi + 1) * POOL_GROUP)
        wsum = v_pool[:, gcols]
        for j in range(1, w):
            wsum = wsum + ext_ref[MAX_WIN - j:MAX_WIN - j + ts, gcols]
        cnt = jnp.minimum(pos + 1, w).astype(F32)
        pooled = wsum / cnt - v_pool[:, gcols]
        yb_ref[:, gi * out_group:(gi + 1) * out_group] = jnp.dot(
            pooled.astype(BF16), pw_ref[gi], preferred_element_type=F32)
    ext_ref[0:MAX_WIN, :] = ext_ref[ts:ts + MAX_WIN, :]
    y_b = yb_ref[...] * ps_ref[...]

    gate_a = proj(4 * width + pool_width, 4 * width + pool_width + d)
    gate_b = proj(4 * width + pool_width + d, 4 * width + pool_width + 2 * d)
    mix = _sigmoid(gate_a) * y_a + _sigmoid(gate_b) * y_b
    y = alpha * h0 + jnp.dot(mix.astype(BF16), wout_ref[...], preferred_element_type=F32)
    h1_ref[...] = _layer_norm(y, ln1g_ref[...], ln1b_ref[...])


def _const_spec(shape):
    nd = len(shape)
    return pl.BlockSpec(shape, lambda *_: (0,) * nd, pipeline_mode=pl.Buffered(1))


def _mixer(x2, batch, seq, alpha, ln0_g, ln0_b, w_in, hg_lb, norm_g, w_hg, pool_w, pool_scale,
           w_out, ln1_g, ln1_b):
    t, d = x2.shape
    ts = MIXER_TOKENS
    ns = seq // ts
    width = HG_HEADS * HG_DK
    pool_width = len(POOL_WINDOWS) * POOL_GROUP
    row = lambda a: a.reshape(1, -1)
    args = (x2, row(ln0_g), row(ln0_b), w_in.astype(BF16), hg_lb, row(norm_g), w_hg.astype(BF16),
            pool_w.astype(BF16), row(pool_scale), w_out.astype(BF16), row(ln1_g), row(ln1_b))
    in_specs = [pl.BlockSpec((ts, d), lambda b, s: (b * ns + s, 0))]
    in_specs += [_const_spec(a.shape) for a in args[1:]]
    return pl.pallas_call(
        functools.partial(_mixer_kernel, alpha),
        grid=(batch, ns),
        in_specs=in_specs,
        out_specs=pl.BlockSpec((ts, d), lambda b, s: (b * ns + s, 0)),
        out_shape=jax.ShapeDtypeStruct((t, d), F32),
        scratch_shapes=[
            pltpu.VMEM((HG_HEADS, HG_DK, HG_DK), F32),
            pltpu.VMEM((ts + MAX_WIN, pool_width), F32),
            pltpu.VMEM((ts, width), BF16),
            pltpu.VMEM((ts, width), BF16),
            pltpu.VMEM((ts, width), BF16),
            pltpu.VMEM((ts, width), BF16),
            pltpu.VMEM((ts // HG_CHUNK, width), F32),
            pltpu.VMEM((ts, width), BF16),
            pltpu.VMEM((ts, d), F32),
            pltpu.VMEM((ts, width), F32),
        ],
        compiler_params=pltpu.CompilerParams(
            dimension_semantics=("arbitrary", "arbitrary"),
            vmem_limit_bytes=48 * 1024 * 1024),
        name="mixer",
    )(*args)


def _sort_network(n):
    def merge(lo, hi, r):
        step = r * 2
        if step < hi - lo:
            yield from merge(lo, hi, step)
            yield from merge(lo + r, hi, step)
            yield from [(i, i + r) for i in range(lo + r, hi - r, step)]
        else:
            yield (lo, lo + r)

    def sort(lo, hi):
        if hi - lo >= 1:
            mid = lo + (hi - lo) // 2
            yield from sort(lo, mid)
            yield from sort(mid + 1, hi)
            yield from merge(lo, hi, 1)

    return tuple(sort(0, n - 1))


def _bitonic_merge_network(n):
    pairs, dist = [], n // 2
    while dist:
        pairs += [(i, i + dist) for i in range(n) if not i & dist]
        dist //= 2
    return tuple(pairs)


def _ordered(a, b):
    a_first = (a[0] > b[0]) | ((a[0] == b[0]) & (a[1] < b[1]))
    first = (jnp.maximum(a[0], b[0]),) + tuple(jnp.where(a_first, x, y) for x, y in zip(a[1:], b[1:]))
    second = (jnp.minimum(a[0], b[0]),) + tuple(jnp.where(a_first, y, x) for x, y in zip(a[1:], b[1:]))
    return first, second


def _top_k_sorted(elems, k):
    return _merge_sorted_blocks([_apply_network(elems[s:s + k], _sort_network(k))
                                 for s in range(0, len(elems), k)])


def _apply_network(elems, network):
    elems = list(elems)
    for i, j in network:
        elems[i], elems[j] = _ordered(elems[i], elems[j])
    return elems


def _merge_sorted_blocks(blocks):
    k = len(blocks[0])
    while len(blocks) > 1:
        blocks = [_apply_network([_ordered(a[i], b[k - 1 - i])[0] for i in range(k)], _bitonic_merge_network(k))
                  for a, b in zip(blocks[0::2], blocks[1::2])]
    return blocks[0]


def _router_kernel(alpha, h_ref, p_ref, wq_ref, keys_ref, wpg_ref, wpp_ref, ids_ref, gates_ref, base_ref,
                   q_ref, sc_ref, ids_s, gates_s):
    h = h_ref[...]
    hb = h.astype(BF16)
    tb = h.shape[0]
    k = PEER_TOPK
    planes = tb // LANES
    vreg = (planes, LANES)
    ple = _sigmoid(jnp.dot(hb, wpg_ref[...], preferred_element_type=F32)) * jnp.dot(
        p_ref[...].astype(BF16), wpp_ref[...], preferred_element_type=F32)
    base_ref[...] = alpha * h + ple

    for hd in range(PEER_HEADS):
        cols = slice(hd * 2 * PEER_HALF, (hd + 1) * 2 * PEER_HALF)
        q = jnp.dot(hb, wq_ref[:, cols], preferred_element_type=F32).astype(BF16)
        q_ref[2 * hd] = q[:, :PEER_HALF]
        q_ref[2 * hd + 1] = q[:, PEER_HALF:]

    def per_key_rows(ref, row):
        return ref.at[pl.ds(row, planes, stride=LANES), :]

    def head(hd, carry):
        tops = []
        for half in range(2):
            sc = _dot_nt(keys_ref[hd, half], q_ref[2 * hd + half])
            for pln in range(planes):
                sc_ref[pln * LANES:(pln + 1) * LANES, :] = sc[:, pln * LANES:(pln + 1) * LANES]
            elems = [(per_key_rows(sc_ref, key)[...], jnp.full(vreg, float(key), F32))
                     for key in range(PEER_NKEYS)]
            tops.append(_top_k_sorted(elems, k))
        (top1, top2) = tops
        def cand(a, b):
            return (top1[a][0] + top2[b][0], jnp.full(vreg, float(a * k + b), F32),
                    top1[a][1] * PEER_NKEYS + top2[b][1])

        def pad(i):
            return (jnp.full(vreg, -jnp.inf, F32), jnp.full(vreg, float(k * k + i), F32), jnp.zeros(vreg, F32))

        rows = [[cand(a, b) for b in range(k // (a + 1))] for a in range(k)]
        singles = [rows[a][0] for a in range(k // 2, k)]
        mixed = [c for a in range(2, k // 2 - 1) for c in rows[a]]
        last = rows[k // 2 - 1] + [pad(i) for i in range(k - len(rows[k // 2 - 1]))]
        assert len(rows[0]) == k and len(rows[1]) + len(singles) == k and len(mixed) == k
        best = _merge_sorted_blocks([
            rows[0],
            _apply_network(rows[1] + singles[::-1], _bitonic_merge_network(k)),
            _apply_network(mixed, _sort_network(k)),
            last])
        ex = [jnp.exp(e[0] - best[0][0]) for e in best]
        denom = functools.reduce(jnp.add, ex)
        for j in range(k):
            slot = hd * k + j
            per_key_rows(gates_s, slot)[...] = ex[j] / denom
            per_key_rows(ids_s, slot)[...] = best[j][2] * PACK_ROWS
        return carry

    lax.fori_loop(0, PEER_HEADS, head, 0)
    for pln in range(planes):
        rows = slice(pln * LANES, (pln + 1) * LANES)
        gates_ref[rows, :] = gates_s[rows, :].T
        ids_ref[rows, :] = ids_s[rows, :].T.astype(jnp.int32)


def _router(h1, p2, alpha, w_query, sub_keys, w_ple_gate, w_ple_proj):
    t, d = h1.shape
    tb = ROUTER_TOKENS
    slots = PEER_HEADS * PEER_TOPK
    assert slots == LANES and PEER_NKEYS == LANES
    args = (h1, p2, w_query.astype(BF16), sub_keys.astype(BF16), w_ple_gate.astype(BF16),
            w_ple_proj.astype(BF16))
    in_specs = [pl.BlockSpec((tb, d), lambda i: (i, 0)),
                pl.BlockSpec((tb, p2.shape[1]), lambda i: (i, 0))]
    in_specs += [_const_spec(a.shape) for a in args[2:]]
    return pl.pallas_call(
        functools.partial(_router_kernel, alpha),
        grid=(t // tb,),
        in_specs=in_specs,
        out_specs=[pl.BlockSpec((tb, slots), lambda i: (i, 0)),
                   pl.BlockSpec((tb, slots), lambda i: (i, 0)),
                   pl.BlockSpec((tb, d), lambda i: (i, 0))],
        out_shape=[jax.ShapeDtypeStruct((t, slots), jnp.int32),
                   jax.ShapeDtypeStruct((t, slots), F32),
                   jax.ShapeDtypeStruct((t, d), F32)],
        scratch_shapes=[pltpu.VMEM((2 * PEER_HEADS, tb, PEER_HALF), BF16),
                        pltpu.VMEM((tb, LANES), F32),
                        pltpu.VMEM((tb, LANES), F32),
                        pltpu.VMEM((tb, LANES), F32)],
        compiler_params=pltpu.CompilerParams(
            dimension_semantics=("arbitrary",),
            vmem_limit_bytes=56 * 1024 * 1024),
        name="router",
    )(*args)


def _pack_kernel(tab_ref, out_ref):
    half = tab_ref.shape[1] // 2
    words = pltpu.pack_elementwise([tab_ref[:, :half], tab_ref[:, half:]], packed_dtype=BF16)
    out_ref[...] = pltpu.bitcast(words, jnp.uint32)


def _pack_table(tab):
    n, d = tab.shape
    rows = PACK_BLOCK_ROWS
    words = pl.pallas_call(
        _pack_kernel,
        grid=(n // rows,),
        in_specs=[pl.BlockSpec((rows, d), lambda i: (i, 0))],
        out_specs=pl.BlockSpec((rows, d // 2), lambda i: (i, 0)),
        out_shape=jax.ShapeDtypeStruct((n, d // 2), jnp.uint32),
        name="pack",
    )(tab)
    return words.reshape(n * PACK_ROWS, LANES)


def _gather_rows(ids_ref, tab_ref, g_ref, t, slots):
    ids_t = ids_ref.at[pl.ds(t * slots, slots)]
    for k in range(slots):
        slab = tab_ref[pl.ds(pl.multiple_of(ids_t[k], PACK_ROWS), PACK_ROWS), :]
        start = (k // SUBLANES) * (PACK_ROWS * SUBLANES) + k % SUBLANES
        g_ref[pl.ds(start, PACK_ROWS, stride=SUBLANES), :] = slab


def _word_tile(g_ref, slots, c):
    return jnp.concatenate(
        [g_ref[(kt * PACK_ROWS + c) * SUBLANES:(kt * PACK_ROWS + c + 1) * SUBLANES, :]
         for kt in range(slots // SUBLANES)], axis=0)


def _gathered_matrix(g_ref, slots):
    words = jnp.concatenate([_word_tile(g_ref, slots, c) for c in range(PACK_ROWS)], axis=1)
    return pltpu.bitcast(words, BF16)


def _pipelined_tokens(tb, gather, compute, bufs):
    g0, g1 = bufs
    gather(0, g0)
    gather(1, g1)

    def pair(i, carry):
        t = 2 * i
        compute(t, g0)
        gather(t + 2, g0)
        compute(t + 1, g1)
        gather(t + 3, g1)
        return carry

    lax.fori_loop(0, tb // 2 - 1, pair, 0)
    compute(tb - 2, g0)
    compute(tb - 1, g1)


def _pair_select(slots):
    r = lax.broadcasted_iota(jnp.int32, (2 * slots, slots), 0)
    c = lax.broadcasted_iota(jnp.int32, (2 * slots, slots), 1)
    return (r // 2 == c).astype(F32)


def _gelu(x):
    return 0.5 * x * (1.0 + lax.erf(x * (2.0 ** -0.5)))


def _split_bf16(x):
    head = x.astype(BF16).astype(F32)
    return head, x - head


def _store_rows(ref, j, tb, value):
    for c in range(ref.shape[0]):
        ref[c, pl.ds(j, tb, stride=SUBLANES), :] = value[:, c * LANES:(c + 1) * LANES]


def _load_rows(ref, j, tb):
    return jnp.concatenate([ref[c, pl.ds(j, tb, stride=SUBLANES), :] for c in range(ref.shape[0])], axis=1)


def _load_group(ref, rows):
    return jnp.concatenate([ref[c, rows, :] for c in range(ref.shape[0])], axis=1)


def _store_group(ref, rows, value):
    for c in range(ref.shape[0]):
        ref[c, rows, :] = value[:, c * LANES:(c + 1) * LANES]


def _peer_u_kernel(ids_ref, h_ref, gate_ref, tab_ref, act_ref, g0_ref, g1_ref, lhs_ref, r_ref):
    tb, d = h_ref.shape
    slots = gate_ref.shape[1]
    half = d // 2
    parts = 4

    @pl.when(pl.program_id(0) == 0)
    def _():
        lhs_ref[...] = jnp.zeros_like(lhs_ref)

    x_hi, x_lo = _split_bf16(h_ref[...])
    for j, part in enumerate((x_hi[:, :half], x_lo[:, :half], x_hi[:, half:], x_lo[:, half:])):
        _store_rows(lhs_ref, j, tb, part)

    def gather(t, g_ref):
        _gather_rows(ids_ref, tab_ref, g_ref, t, slots)

    def compute(t, g_ref):
        rows = pl.ds(pl.multiple_of(t * SUBLANES, SUBLANES), SUBLANES)
        _store_group(r_ref, rows, _dot_nt(_load_group(lhs_ref, rows).astype(BF16), _gathered_matrix(g_ref, slots)))

    _pipelined_tokens(tb, gather, compute, (g0_ref, g1_ref))

    r = [_load_rows(r_ref, j, tb) for j in range(parts)]
    lane_even = lax.broadcasted_iota(jnp.int32, (1, 2 * slots), 1) % 2 == 0
    inter = jnp.where(lane_even, r[0] + r[1], r[2] + r[3])
    pre = jnp.dot(inter, _pair_select(slots), preferred_element_type=F32, precision=lax.Precision.HIGHEST)
    act_ref[...] = _gelu(pre) * gate_ref[...]


def _peer_v_kernel(ids_ref, act_ref, base_ref, tab_ref, lng_ref, lnb_ref, out_ref, g0_ref, g1_ref, lhs_ref,
                   r_ref):
    tb, d = base_ref.shape
    slots = act_ref.shape[1]

    @pl.when(pl.program_id(0) == 0)
    def _():
        lhs_ref[...] = jnp.zeros_like(lhs_ref)

    a2 = lax.dot_general(act_ref[...], _pair_select(slots), (((1,), (1,)), ((), ())),
                         preferred_element_type=F32, precision=lax.Precision.HIGHEST)
    lane_even = lax.broadcasted_iota(jnp.int32, (1, 2 * slots), 1) % 2 == 0
    a_hi, a_lo = _split_bf16(a2)
    for j, part in enumerate((jnp.where(lane_even, a_hi, 0.0), jnp.where(lane_even, a_lo, 0.0),
                              jnp.where(lane_even, 0.0, a_hi), jnp.where(lane_even, 0.0, a_lo))):
        _store_rows(lhs_ref, j, tb, part)

    def gather(t, g_ref):
        _gather_rows(ids_ref, tab_ref, g_ref, t, slots)

    def compute(t, g_ref):
        rows = pl.ds(pl.multiple_of(t * SUBLANES, SUBLANES), SUBLANES)
        _store_group(r_ref, rows, jnp.dot(_load_group(lhs_ref, rows).astype(BF16),
                                          _gathered_matrix(g_ref, slots), preferred_element_type=F32))

    _pipelined_tokens(tb, gather, compute, (g0_ref, g1_ref))

    r = [_load_rows(r_ref, j, tb) for j in range(4)]
    ffn = jnp.concatenate([r[0] + r[1], r[2] + r[3]], axis=1)
    out_ref[...] = _layer_norm(base_ref[...] + ffn, lng_ref[...], lnb_ref[...])


def _peer_specs(tb, d, slots, tab):
    ids_spec = pl.BlockSpec((tb * slots,), lambda i: (i,), memory_space=pltpu.SMEM)
    tile_spec = pl.BlockSpec((tb, d), lambda i: (i, 0))
    slot_spec = pl.BlockSpec((tb, slots), lambda i: (i, 0))
    return ids_spec, tile_spec, slot_spec, _const_spec(tab.shape)


def _peer_u(ids_flat, h1, gates, tab):
    t, d = h1.shape
    slots = gates.shape[1]
    tb = PEER_TOKENS
    ids_spec, tile_spec, slot_spec, tab_spec = _peer_specs(tb, d, slots, tab)
    return pl.pallas_call(
        _peer_u_kernel,
        grid=(t // tb,),
        in_specs=[ids_spec, tile_spec, slot_spec, tab_spec],
        out_specs=slot_spec,
        out_shape=jax.ShapeDtypeStruct((t, slots), F32),
        scratch_shapes=[
            *[pltpu.VMEM((slots * PACK_ROWS, LANES), jnp.uint32)] * 2,
            pltpu.VMEM((d // 2 // LANES, tb * SUBLANES, LANES), F32),
            pltpu.VMEM((2 * slots // LANES, tb * SUBLANES, LANES), F32),
        ],
        compiler_params=pltpu.CompilerParams(
            dimension_semantics=("arbitrary",),
            vmem_limit_bytes=52 * 1024 * 1024),
        name="peer_u",
    )(ids_flat, h1, gates, tab)


def _peer_v(ids_flat, act, base, tab, ln_g, ln_b):
    t, d = base.shape
    slots = act.shape[1]
    tb = PEER_TOKENS
    ids_spec, tile_spec, slot_spec, tab_spec = _peer_specs(tb, d, slots, tab)
    row = lambda a: a.reshape(1, -1)
    return pl.pallas_call(
        _peer_v_kernel,
        grid=(t // tb,),
        in_specs=[ids_spec, slot_spec, tile_spec, tab_spec, _const_spec((1, d)), _const_spec((1, d))],
        out_specs=tile_spec,
        out_shape=jax.ShapeDtypeStruct((t, d), F32),
        scratch_shapes=[
            *[pltpu.VMEM((slots * PACK_ROWS, LANES), jnp.uint32)] * 2,
            pltpu.VMEM((2 * slots // LANES, tb * SUBLANES, LANES), F32),
            pltpu.VMEM((d // 2 // LANES, tb * SUBLANES, LANES), F32),
        ],
        compiler_params=pltpu.CompilerParams(
            dimension_semantics=("arbitrary",),
            vmem_limit_bytes=52 * 1024 * 1024),
        name="peer_v",
    )(ids_flat, act, base, tab, row(ln_g), row(ln_b))


def kernel(x, p, ln0_g, ln0_b, w_in, hg_lb, hg_norm_g, w_hg_branch, pool_w, pool_scale, w_out,
           ln1_g, ln1_b, w_query, sub_keys, u_tab, v_tab, w_ple_gate, w_ple_proj, ln2_g, ln2_b):
    batch, seq, d = x.shape
    depth = w_in.shape[0]
    t = batch * seq
    assert depth == 1 and seq % MIXER_TOKENS == 0 and t % ROUTER_TOKENS == 0 and t % PEER_TOKENS == 0
    alpha = (2.0 * depth) ** 0.25
    h1 = _mixer(x.reshape(t, d), batch, seq, alpha, ln0_g, ln0_b, w_in[0], hg_lb, hg_norm_g[0],
                w_hg_branch[0], pool_w[0], pool_scale[0], w_out[0], ln1_g[0], ln1_b[0])
    ids, gates, base = _router(h1, p[0].reshape(t, -1), alpha, w_query[0], sub_keys[0],
                               w_ple_gate[0], w_ple_proj[0])
    ids_flat = ids.reshape(-1)
    act = _peer_u(ids_flat, h1, gates, _pack_table(u_tab[0]))
    out = _peer_v(ids_flat, act, base, _pack_table(v_tab[0]), ln2_g[0], ln2_b[0])
    return out.reshape(batch, seq, d)
```

```python
import functools

import jax
import jax.numpy as jnp
from jax import lax
from jax.experimental import pallas as pl
from jax.experimental.pallas import tpu as pltpu

F32 = jnp.float32
BF16 = jnp.bfloat16

HG_HEADS = 8
HG_DK = 128
HG_CHUNK = 32
POOL_WINDOWS = (2, 4, 8, 16)
POOL_GROUP = 128
MAX_WIN = 16
PEER_HEADS = 8
PEER_NKEYS = 128
PEER_HALF = 128
PEER_TOPK = 16
LN_EPS = 1e-5
RMS_EPS = 1e-6

LANES = 128
SUBLANES = 8

MIXER_TOKENS = 256
ROUTER_TOKENS = 1024
PEER_TOKENS = 256
PACK_ROWS = 4
PACK_BLOCK_ROWS = 512


def _layer_norm(x, g, b):
    mu = jnp.mean(x, axis=-1, keepdims=True)
    xc = x - mu
    var = jnp.mean(xc * xc, axis=-1, keepdims=True)
    return xc * lax.rsqrt(var + LN_EPS) * g + b


def _sigmoid(x):
    return 1.0 / (1.0 + jnp.exp(-x))


def _dot_nt(a, b):
    return lax.dot_general(a, b, (((1,), (1,)), ((), ())), preferred_element_type=F32)


def _dot_tn(a, b):
    return lax.dot_general(a, b, (((0,), (0,)), ((), ())), preferred_element_type=F32)


def _mixer_kernel(alpha, x_ref, ln0g_ref, ln0b_ref, win_ref, hglb_ref, ng_ref, whg_ref, pw_ref,
                  ps_ref, wout_ref, ln1g_ref, ln1b_ref, h1_ref,
                  st_ref, ext_ref, qd_ref, ki_ref, ke_ref, vv_ref, dec_ref, o_ref, yb_ref, oi_ref):
    ts, d = x_ref.shape
    nc = ts // HG_CHUNK
    width = HG_HEADS * HG_DK
    pool_width = len(POOL_WINDOWS) * POOL_GROUP
    s_idx = pl.program_id(1)

    @pl.when(s_idx == 0)
    def _():
        st_ref[...] = jnp.zeros_like(st_ref)
        ext_ref[0:MAX_WIN, :] = jnp.zeros((MAX_WIN, pool_width), F32)

    h0 = _layer_norm(x_ref[...], ln0g_ref[...], ln0b_ref[...])
    hb = h0.astype(BF16)

    def proj(lo, hi):
        return jnp.dot(hb, win_ref[:, lo:hi], preferred_element_type=F32)

    lbl = hglb_ref[...]
    lmax = jnp.max(lbl, axis=0, keepdims=True)
    lexp = jnp.exp(lbl - lmax)
    lb = lexp[0:1, :] / jnp.sum(lexp, axis=0, keepdims=True)

    f = lb + (1.0 - lb) * _sigmoid(proj(width, 2 * width))
    kk = 1.0 - f
    bh = jnp.log(f)
    row_in_chunk = lax.broadcasted_iota(jnp.int32, (ts, 1), 0) % HG_CHUNK
    sh = 1
    while sh < HG_CHUNK:
        bh = bh + jnp.where(row_in_chunk >= sh, pltpu.roll(bh, sh, axis=0), 0.0)
        sh *= 2
    bh3 = bh.reshape(nc, HG_CHUNK, width)
    bl3 = bh3[:, HG_CHUNK - 1:HG_CHUNK, :]
    dec_ref[...] = jnp.exp(bl3.reshape(nc, width))
    ke_ref[...] = (kk * jnp.exp(jnp.broadcast_to(bl3, bh3.shape).reshape(ts, width) - bh)).astype(BF16)
    ki_ref[...] = (kk * jnp.exp(-bh)).astype(BF16)
    q = proj(0, width)
    qd_ref[...] = (q * _sigmoid(q) * (HG_DK ** -0.5) * jnp.exp(bh)).astype(BF16)
    vv_ref[...] = proj(2 * width, 3 * width).astype(BF16)
    g_out = proj(3 * width, 4 * width)
    g_act = g_out * _sigmoid(g_out)

    r_i = lax.broadcasted_iota(jnp.int32, (ts, ts), 0)
    c_i = lax.broadcasted_iota(jnp.int32, (ts, ts), 1)
    causal = (r_i // HG_CHUNK == c_i // HG_CHUNK) & (c_i <= r_i)

    for c in range(nc):
        rows = slice(c * HG_CHUNK, (c + 1) * HG_CHUNK)
        for h in range(HG_HEADS):
            cols = slice(h * HG_DK, (h + 1) * HG_DK)
            st = st_ref[h]
            oi_ref[rows, cols] = _dot_nt(qd_ref[rows, cols], st.astype(BF16))
            st_ref[h] = st * dec_ref[c:c + 1, cols] + _dot_tn(vv_ref[rows, cols], ke_ref[rows, cols])

    for h in range(HG_HEADS):
        cols = slice(h * HG_DK, (h + 1) * HG_DK)
        scores = jnp.where(causal, _dot_nt(qd_ref[:, cols], ki_ref[:, cols]), 0.0)
        o_h = jnp.dot(scores.astype(BF16), vv_ref[:, cols], preferred_element_type=F32) + oi_ref[:, cols]
        o_h = o_h * lax.rsqrt(jnp.mean(o_h * o_h, axis=-1, keepdims=True) + RMS_EPS) * ng_ref[...]
        o_ref[:, cols] = (o_h * g_act[:, cols]).astype(BF16)
    y_a = jnp.dot(o_ref[...], whg_ref[...], preferred_element_type=F32)

    v_pool = proj(4 * width, 4 * width + pool_width)
    ext_ref[MAX_WIN:MAX_WIN + ts, :] = v_pool
    pos = s_idx * ts + lax.broadcasted_iota(jnp.int32, (ts, 1), 0)
    out_group = d // len(POOL_WINDOWS)
    for gi, w in enumerate(POOL_WINDOWS):
        gcols = slice(gi * POOL_GROUP, (gi + 1) * POOL_GROUP)
        wsum = v_pool[:, gcols]
        for j in range(1, w):
            wsum = wsum + ext_ref[MAX_WIN - j:MAX_WIN - j + ts, gcols]
        cnt = jnp.minimum(pos + 1, w).astype(F32)
        pooled = wsum / cnt - v_pool[:, gcols]
        yb_ref[:, gi * out_group:(gi + 1) * out_group] = jnp.dot(
            pooled.astype(BF16), pw_ref[gi], preferred_element_type=F32)
    ext_ref[0:MAX_WIN, :] = ext_ref[ts:ts + MAX_WIN, :]
    y_b = yb_ref[...] * ps_ref[...]

    gate_a = proj(4 * width + pool_width, 4 * width + pool_width + d)
    gate_b = proj(4 * width + pool_width + d, 4 * width + pool_width + 2 * d)
    mix = _sigmoid(gate_a) * y_a + _sigmoid(gate_b) * y_b
    y = alpha * h0 + jnp.dot(mix.astype(BF16), wout_ref[...], preferred_element_type=F32)
    h1_ref[...] = _layer_norm(y, ln1g_ref[...], ln1b_ref[...])


def _const_spec(shape):
    nd = len(shape)
    return pl.BlockSpec(shape, lambda *_: (0,) * nd, pipeline_mode=pl.Buffered(1))


def _mixer(x2, batch, seq, alpha, ln0_g, ln0_b, w_in, hg_lb, norm_g, w_hg, pool_w, pool_scale,
           w_out, ln1_g, ln1_b):
    t, d = x2.shape
    ts = MIXER_TOKENS
    ns = seq // ts
    width = HG_HEADS * HG_DK
    pool_width = len(POOL_WINDOWS) * POOL_GROUP
    row = lambda a: a.reshape(1, -1)
    args = (x2, row(ln0_g), row(ln0_b), w_in.astype(BF16), hg_lb, row(norm_g), w_hg.astype(BF16),
            pool_w.astype(BF16), row(pool_scale), w_out.astype(BF16), row(ln1_g), row(ln1_b))
    in_specs = [pl.BlockSpec((ts, d), lambda b, s: (b * ns + s, 0))]
    in_specs += [_const_spec(a.shape) for a in args[1:]]
    return pl.pallas_call(
        functools.partial(_mixer_kernel, alpha),
        grid=(batch, ns),
        in_specs=in_specs,
        out_specs=pl.BlockSpec((ts, d), lambda b, s: (b * ns + s, 0)),
        out_shape=jax.ShapeDtypeStruct((t, d), F32),
        scratch_shapes=[
            pltpu.VMEM((HG_HEADS, HG_DK, HG_DK), F32),
            pltpu.VMEM((ts + MAX_WIN, pool_width), F32),
            pltpu.VMEM((ts, width), BF16),
            pltpu.VMEM((ts, width), BF16),
            pltpu.VMEM((ts, width), BF16),
            pltpu.VMEM((ts, width), BF16),
            pltpu.VMEM((ts // HG_CHUNK, width), F32),
            pltpu.VMEM((ts, width), BF16),
            pltpu.VMEM((ts, d), F32),
            pltpu.VMEM((ts, width), F32),
        ],
        compiler_params=pltpu.CompilerParams(
            dimension_semantics=("arbitrary", "arbitrary"),
            vmem_limit_bytes=48 * 1024 * 1024),
        name="mixer",
    )(*args)


def _sort_network(n):
    def merge(lo, hi, r):
        step = r * 2
        if step < hi - lo:
            yield from merge(lo, hi, step)
            yield from merge(lo + r, hi, step)
            yield from [(i, i + r) for i in range(lo + r, hi - r, step)]
        else:
            yield (lo, lo + r)

    def sort(lo, hi):
        if hi - lo >= 1:
            mid = lo + (hi - lo) // 2
            yield from sort(lo, mid)
            yield from sort(mid + 1, hi)
            yield from merge(lo, hi, 1)

    return tuple(sort(0, n - 1))


def _bitonic_merge_network(n):
    pairs, dist = [], n // 2
    while dist:
        pairs += [(i, i + dist) for i in range(n) if not i & dist]
        dist //= 2
    return tuple(pairs)


def _ordered(a, b):
    a_first = (a[0] > b[0]) | ((a[0] == b[0]) & (a[1] < b[1]))
    first = (jnp.maximum(a[0], b[0]),) + tuple(jnp.where(a_first, x, y) for x, y in zip(a[1:], b[1:]))
    second = (jnp.minimum(a[0], b[0]),) + tuple(jnp.where(a_first, y, x) for x, y in zip(a[1:], b[1:]))
    return first, second


def _top_k_sorted(elems, k):
    return _merge_sorted_blocks([_apply_network(elems[s:s + k], _sort_network(k))
                                 for s in range(0, len(elems), k)])


def _apply_network(elems, network):
    elems = list(elems)
    for i, j in network:
        elems[i], elems[j] = _ordered(elems[i], elems[j])
    return elems


def _merge_sorted_blocks(blocks):
    k = len(blocks[0])
    while len(blocks) > 1:
        blocks = [_apply_network([_ordered(a[i], b[k - 1 - i])[0] for i in range(k)], _bitonic_merge_network(k))
                  for a, b in zip(blocks[0::2], blocks[1::2])]
    return blocks[0]


def _router_kernel(alpha, h_ref, p_ref, wq_ref, keys_ref, wpg_ref, wpp_ref, ids_ref, gates_ref, base_ref,
                   q_ref, sc_ref, ids_s, gates_s):
    h = h_ref[...]
    hb = h.astype(BF16)
    tb = h.shape[0]
    k = PEER_TOPK
    planes = tb // LANES
    vreg = (planes, LANES)
    ple = _sigmoid(jnp.dot(hb, wpg_ref[...], preferred_element_type=F32)) * jnp.dot(
        p_ref[...].astype(BF16), wpp_ref[...], preferred_element_type=F32)
    base_ref[...] = alpha * h + ple

    for hd in range(PEER_HEADS):
        cols = slice(hd * 2 * PEER_HALF, (hd + 1) * 2 * PEER_HALF)
        q = jnp.dot(hb, wq_ref[:, cols], preferred_element_type=F32).astype(BF16)
        q_ref[2 * hd] = q[:, :PEER_HALF]
        q_ref[2 * hd + 1] = q[:, PEER_HALF:]

    def per_key_rows(ref, row):
        return ref.at[pl.ds(row, planes, stride=LANES), :]

    def head(hd, carry):
        tops = []
        for half in range(2):
            sc = _dot_nt(keys_ref[hd, half], q_ref[2 * hd + half])
            for pln in range(planes):
                sc_ref[pln * LANES:(pln + 1) * LANES, :] = sc[:, pln * LANES:(pln + 1) * LANES]
            elems = [(per_key_rows(sc_ref, key)[...], jnp.full(vreg, float(key), F32))
                     for key in range(PEER_NKEYS)]
            tops.append(_top_k_sorted(elems, k))
        (top1, top2) = tops
        def cand(a, b):
            return (top1[a][0] + top2[b][0], jnp.full(vreg, float(a * k + b), F32),
                    top1[a][1] * PEER_NKEYS + top2[b][1])

        def pad(i):
            return (jnp.full(vreg, -jnp.inf, F32), jnp.full(vreg, float(k * k + i), F32), jnp.zeros(vreg, F32))

        rows = [[cand(a, b) for b in range(k // (a + 1))] for a in range(k)]
        singles = [rows[a][0] for a in range(k // 2, k)]
        mixed = [c for a in range(2, k // 2 - 1) for c in rows[a]]
        last = rows[k // 2 - 1] + [pad(i) for i in range(k - len(rows[k // 2 - 1]))]
        assert len(rows[0]) == k and len(rows[1]) + len(singles) == k and len(mixed) == k
        best = _merge_sorted_blocks([
            rows[0],
            _apply_network(rows[1] + singles[::-1], _bitonic_merge_network(k)),
            _apply_network(mixed, _sort_network(k)),
            last])
        ex = [jnp.exp(e[0] - best[0][0]) for e in best]
        denom = functools.reduce(jnp.add, ex)
        for j in range(k):
            slot = hd * k + j
            per_key_rows(gates_s, slot)[...] = ex[j] / denom
            per_key_rows(ids_s, slot)[...] = best[j][2] * PACK_ROWS
        return carry

    lax.fori_loop(0, PEER_HEADS, head, 0)
    for pln in range(planes):
        rows = slice(pln * LANES, (pln + 1) * LANES)
        gates_ref[rows, :] = gates_s[rows, :].T
        ids_ref[rows, :] = ids_s[rows, :].T.astype(jnp.int32)


def _router(h1, p2, alpha, w_query, sub_keys, w_ple_gate, w_ple_proj):
    t, d = h1.shape
    tb = ROUTER_TOKENS
    slots = PEER_HEADS * PEER_TOPK
    assert slots == LANES and PEER_NKEYS == LANES
    args = (h1, p2, w_query.astype(BF16), sub_keys.astype(BF16), w_ple_gate.astype(BF16),
            w_ple_proj.astype(BF16))
    in_specs = [pl.BlockSpec((tb, d), lambda i: (i, 0)),
                pl.BlockSpec((tb, p2.shape[1]), lambda i: (i, 0))]
    in_specs += [_const_spec(a.shape) for a in args[2:]]
    return pl.pallas_call(
        functools.partial(_router_kernel, alpha),
        grid=(t // tb,),
        in_specs=in_specs,
        out_specs=[pl.BlockSpec((tb, slots), lambda i: (i, 0)),
                   pl.BlockSpec((tb, slots), lambda i: (i, 0)),
                   pl.BlockSpec((tb, d), lambda i: (i, 0))],
        out_shape=[jax.ShapeDtypeStruct((t, slots), jnp.int32),
                   jax.ShapeDtypeStruct((t, slots), F32),
                   jax.ShapeDtypeStruct((t, d), F32)],
        scratch_shapes=[pltpu.VMEM((2 * PEER_HEADS, tb, PEER_HALF), BF16),
                        pltpu.VMEM((tb, LANES), F32),
                        pltpu.VMEM((tb, LANES), F32),
                        pltpu.VMEM((tb, LANES), F32)],
        compiler_params=pltpu.CompilerParams(
            dimension_semantics=("arbitrary",),
            vmem_limit_bytes=56 * 1024 * 1024),
        name="router",
    )(*args)


def _pack_kernel(tab_ref, out_ref):
    half = tab_ref.shape[1] // 2
    words = pltpu.bitcast(pltpu.pack_elementwise([tab_ref[:, :half], tab_ref[:, half:]], packed_dtype=BF16),
                          jnp.uint32)
    rows = tab_ref.shape[0]
    for j in range(PACK_ROWS):
        out_ref[pl.ds(j, rows, stride=PACK_ROWS), :] = words[:, j * LANES:(j + 1) * LANES]


def _pack_table(tab):
    n, d = tab.shape
    rows = PACK_BLOCK_ROWS
    return pl.pallas_call(
        _pack_kernel,
        grid=(n // rows,),
        in_specs=[pl.BlockSpec((rows, d), lambda i: (i, 0))],
        out_specs=pl.BlockSpec((rows * PACK_ROWS, LANES), lambda i: (i, 0)),
        out_shape=jax.ShapeDtypeStruct((n * PACK_ROWS, LANES), jnp.uint32),
        name="pack",
    )(tab)


def _gather_rows(ids_ref, tab_ref, g_ref, t, slots):
    ids_t = ids_ref.at[pl.ds(t * slots, slots)]
    for k in range(slots):
        slab = tab_ref[pl.ds(pl.multiple_of(ids_t[k], PACK_ROWS), PACK_ROWS), :]
        start = (k // SUBLANES) * (PACK_ROWS * SUBLANES) + k % SUBLANES
        g_ref[pl.ds(start, PACK_ROWS, stride=SUBLANES), :] = slab


def _word_tile(g_ref, slots, c):
    return jnp.concatenate(
        [g_ref[(kt * PACK_ROWS + c) * SUBLANES:(kt * PACK_ROWS + c + 1) * SUBLANES, :]
         for kt in range(slots // SUBLANES)], axis=0)


def _gathered_matrix(g_ref, slots):
    words = jnp.concatenate([_word_tile(g_ref, slots, c) for c in range(PACK_ROWS)], axis=1)
    return pltpu.bitcast(words, BF16)


def _pipelined_tokens(tb, gather, compute, bufs):
    g0, g1 = bufs
    gather(0, g0)
    gather(1, g1)

    def pair(i, carry):
        t = 2 * i
        compute(t, g0)
        gather(t + 2, g0)
        compute(t + 1, g1)
        gather(t + 3, g1)
        return carry

    lax.fori_loop(0, tb // 2 - 1, pair, 0)
    compute(tb - 2, g0)
    compute(tb - 1, g1)


def _pair_select(slots):
    r = lax.broadcasted_iota(jnp.int32, (2 * slots, slots), 0)
    c = lax.broadcasted_iota(jnp.int32, (2 * slots, slots), 1)
    return (r // 2 == c).astype(F32)


def _gelu(x):
    return 0.5 * x * (1.0 + lax.erf(x * (2.0 ** -0.5)))


def _split_bf16(x):
    head = x.astype(BF16).astype(F32)
    return head, x - head


def _store_rows(ref, j, tb, value):
    for c in range(ref.shape[0]):
        ref[c, pl.ds(j, tb, stride=SUBLANES), :] = value[:, c * LANES:(c + 1) * LANES]


def _load_rows(ref, j, tb):
    return jnp.concatenate([ref[c, pl.ds(j, tb, stride=SUBLANES), :] for c in range(ref.shape[0])], axis=1)


def _load_group(ref, rows):
    return jnp.concatenate([ref[c, rows, :] for c in range(ref.shape[0])], axis=1)


def _store_group(ref, rows, value):
    for c in range(ref.shape[0]):
        ref[c, rows, :] = value[:, c * LANES:(c + 1) * LANES]


def _peer_u_kernel(ids_ref, h_ref, gate_ref, tab_ref, act_ref, g0_ref, g1_ref, lhs_ref, r_ref):
    tb, d = h_ref.shape
    slots = gate_ref.shape[1]
    half = d // 2
    parts = 4

    @pl.when(pl.program_id(0) == 0)
    def _():
        lhs_ref[...] = jnp.zeros_like(lhs_ref)

    x_hi, x_lo = _split_bf16(h_ref[...])
    for j, part in enumerate((x_hi[:, :half], x_lo[:, :half], x_hi[:, half:], x_lo[:, half:])):
        _store_rows(lhs_ref, j, tb, part)

    def gather(t, g_ref):
        _gather_rows(ids_ref, tab_ref, g_ref, t, slots)

    def compute(t, g_ref):
        rows = pl.ds(pl.multiple_of(t * SUBLANES, SUBLANES), SUBLANES)
        _store_group(r_ref, rows, _dot_nt(_load_group(lhs_ref, rows).astype(BF16), _gathered_matrix(g_ref, slots)))

    _pipelined_tokens(tb, gather, compute, (g0_ref, g1_ref))

    r = [_load_rows(r_ref, j, tb) for j in range(parts)]
    lane_even = lax.broadcasted_iota(jnp.int32, (1, 2 * slots), 1) % 2 == 0
    inter = jnp.where(lane_even, r[0] + r[1], r[2] + r[3])
    pre = jnp.dot(inter, _pair_select(slots), preferred_element_type=F32, precision=lax.Precision.HIGHEST)
    act_ref[...] = _gelu(pre) * gate_ref[...]


def _peer_v_kernel(ids_ref, act_ref, base_ref, tab_ref, lng_ref, lnb_ref, out_ref, g0_ref, g1_ref, lhs_ref,
                   r_ref):
    tb, d = base_ref.shape
    slots = act_ref.shape[1]

    @pl.when(pl.program_id(0) == 0)
    def _():
        lhs_ref[...] = jnp.zeros_like(lhs_ref)

    a2 = lax.dot_general(act_ref[...], _pair_select(slots), (((1,), (1,)), ((), ())),
                         preferred_element_type=F32, precision=lax.Precision.HIGHEST)
    lane_even = lax.broadcasted_iota(jnp.int32, (1, 2 * slots), 1) % 2 == 0
    a_hi, a_lo = _split_bf16(a2)
    for j, part in enumerate((jnp.where(lane_even, a_hi, 0.0), jnp.where(lane_even, a_lo, 0.0),
                              jnp.where(lane_even, 0.0, a_hi), jnp.where(lane_even, 0.0, a_lo))):
        _store_rows(lhs_ref, j, tb, part)

    def gather(t, g_ref):
        _gather_rows(ids_ref, tab_ref, g_ref, t, slots)

    def compute(t, g_ref):
        rows = pl.ds(pl.multiple_of(t * SUBLANES, SUBLANES), SUBLANES)
        _store_group(r_ref, rows, jnp.dot(_load_group(lhs_ref, rows).astype(BF16),
                                          _gathered_matrix(g_ref, slots), preferred_element_type=F32))

    _pipelined_tokens(tb, gather, compute, (g0_ref, g1_ref))

    r = [_load_rows(r_ref, j, tb) for j in range(4)]
    ffn = jnp.concatenate([r[0] + r[1], r[2] + r[3]], axis=1)
    out_ref[...] = _layer_norm(base_ref[...] + ffn, lng_ref[...], lnb_ref[...])


def _peer_specs(tb, d, slots, tab):
    ids_spec = pl.BlockSpec((tb * slots,), lambda i: (i,), memory_space=pltpu.SMEM)
    tile_spec = pl.BlockSpec((tb, d), lambda i: (i, 0))
    slot_spec = pl.BlockSpec((tb, slots), lambda i: (i, 0))
    return ids_spec, tile_spec, slot_spec, _const_spec(tab.shape)


def _peer_u(ids_flat, h1, gates, tab):
    t, d = h1.shape
    slots = gates.shape[1]
    tb = PEER_TOKENS
    ids_spec, tile_spec, slot_spec, tab_spec = _peer_specs(tb, d, slots, tab)
    return pl.pallas_call(
        _peer_u_kernel,
        grid=(t // tb,),
        in_specs=[ids_spec, tile_spec, slot_spec, tab_spec],
        out_specs=slot_spec,
        out_shape=jax.ShapeDtypeStruct((t, slots), F32),
        scratch_shapes=[
            *[pltpu.VMEM((slots * PACK_ROWS, LANES), jnp.uint32)] * 2,
            pltpu.VMEM((d // 2 // LANES, tb * SUBLANES, LANES), F32),
            pltpu.VMEM((2 * slots // LANES, tb * SUBLANES, LANES), F32),
        ],
        compiler_params=pltpu.CompilerParams(
            dimension_semantics=("arbitrary",),
            vmem_limit_bytes=52 * 1024 * 1024),
        name="peer_u",
    )(ids_flat, h1, gates, tab)


def _peer_v(ids_flat, act, base, tab, ln_g, ln_b):
    t, d = base.shape
    slots = act.shape[1]
    tb = PEER_TOKENS
    ids_spec, tile_spec, slot_spec, tab_spec = _peer_specs(tb, d, slots, tab)
    row = lambda a: a.reshape(1, -1)
    return pl.pallas_call(
        _peer_v_kernel,
        grid=(t // tb,),
        in_specs=[ids_spec, slot_spec, tile_spec, tab_spec, _const_spec((1, d)), _const_spec((1, d))],
        out_specs=tile_spec,
        out_shape=jax.ShapeDtypeStruct((t, d), F32),
        scratch_shapes=[
            *[pltpu.VMEM((slots * PACK_ROWS, LANES), jnp.uint32)] * 2,
            pltpu.VMEM((2 * slots // LANES, tb * SUBLANES, LANES), F32),
            pltpu.VMEM((d // 2 // LANES, tb * SUBLANES, LANES), F32),
        ],
        compiler_params=pltpu.CompilerParams(
            dimension_semantics=("arbitrary",),
            vmem_limit_bytes=52 * 1024 * 1024),
        name="peer_v",
    )(ids_flat, act, base, tab, row(ln_g), row(ln_b))


def kernel(x, p, ln0_g, ln0_b, w_in, hg_lb, hg_norm_g, w_hg_branch, pool_w, pool_scale, w_out,
           ln1_g, ln1_b, w_query, sub_keys, u_tab, v_tab, w_ple_gate, w_ple_proj, ln2_g, ln2_b):
    batch, seq, d = x.shape
    depth = w_in.shape[0]
    t = batch * seq
    assert depth == 1 and seq % MIXER_TOKENS == 0 and t % ROUTER_TOKENS == 0 and t % PEER_TOKENS == 0
    alpha = (2.0 * depth) ** 0.25
    h1 = _mixer(x.reshape(t, d), batch, seq, alpha, ln0_g, ln0_b, w_in[0], hg_lb, hg_norm_g[0],
                w_hg_branch[0], pool_w[0], pool_scale[0], w_out[0], ln1_g[0], ln1_b[0])
    ids, gates, base = _router(h1, p[0].reshape(t, -1), alpha, w_query[0], sub_keys[0],
                               w_ple_gate[0], w_ple_proj[0])
    ids_flat = ids.reshape(-1)
    act = _peer_u(ids_flat, h1, gates, _pack_table(u_tab[0]))
    out = _peer_v(ids_flat, act, base, _pack_table(v_tab[0]), ln2_g[0], ln2_b[0])
    return out.reshape(batch, seq, d)
```

```python
import functools

import jax
import jax.numpy as jnp
from jax import lax
from jax.experimental import pallas as pl
from jax.experimental.pallas import tpu as pltpu

F32 = jnp.float32
BF16 = jnp.bfloat16

HG_HEADS = 8
HG_DK = 128
HG_CHUNK = 32
POOL_WINDOWS = (2, 4, 8, 16)
POOL_GROUP = 128
MAX_WIN = 16
PEER_HEADS = 8
PEER_NKEYS = 128
PEER_HALF = 128
PEER_TOPK = 16
LN_EPS = 1e-5
RMS_EPS = 1e-6

LANES = 128
SUBLANES = 8

MIXER_TOKENS = 256
ROUTER_TOKENS = 1024
PEER_TOKENS = 256
PACK_ROWS = 4
PACK_BLOCK_ROWS = 512


def _layer_norm(x, g, b):
    mu = jnp.mean(x, axis=-1, keepdims=True)
    xc = x - mu
    var = jnp.mean(xc * xc, axis=-1, keepdims=True)
    return xc * lax.rsqrt(var + LN_EPS) * g + b


def _sigmoid(x):
    return 1.0 / (1.0 + jnp.exp(-x))


def _dot_nt(a, b):
    return lax.dot_general(a, b, (((1,), (1,)), ((), ())), preferred_element_type=F32)


def _dot_tn(a, b):
    return lax.dot_general(a, b, (((0,), (0,)), ((), ())), preferred_element_type=F32)


def _mixer_kernel(alpha, x_ref, ln0g_ref, ln0b_ref, win_ref, hglb_ref, ng_ref, whg_ref, pw_ref,
                  ps_ref, wout_ref, ln1g_ref, ln1b_ref, h1_ref,
                  st_ref, ext_ref, qd_ref, ki_ref, ke_ref, vv_ref, dec_ref, o_ref, yb_ref, oi_ref):
    ts, d = x_ref.shape
    nc = ts // HG_CHUNK
    width = HG_HEADS * HG_DK
    pool_width = len(POOL_WINDOWS) * POOL_GROUP
    s_idx = pl.program_id(1)

    @pl.when(s_idx == 0)
    def _():
        st_ref[...] = jnp.zeros_like(st_ref)
        ext_ref[0:MAX_WIN, :] = jnp.zeros((MAX_WIN, pool_width), F32)

    h0 = _layer_norm(x_ref[...], ln0g_ref[...], ln0b_ref[...])
    hb = h0.astype(BF16)

    def proj(lo, hi):
        return jnp.dot(hb, win_ref[:, lo:hi], preferred_element_type=F32)

    lbl = hglb_ref[...]
    lmax = jnp.max(lbl, axis=0, keepdims=True)
    lexp = jnp.exp(lbl - lmax)
    lb = lexp[0:1, :] / jnp.sum(lexp, axis=0, keepdims=True)

    f = lb + (1.0 - lb) * _sigmoid(proj(width, 2 * width))
    kk = 1.0 - f
    bh = jnp.log(f)
    row_in_chunk = lax.broadcasted_iota(jnp.int32, (ts, 1), 0) % HG_CHUNK
    sh = 1
    while sh < HG_CHUNK:
        bh = bh + jnp.where(row_in_chunk >= sh, pltpu.roll(bh, sh, axis=0), 0.0)
        sh *= 2
    bh3 = bh.reshape(nc, HG_CHUNK, width)
    bl3 = bh3[:, HG_CHUNK - 1:HG_CHUNK, :]
    dec_ref[...] = jnp.exp(bl3.reshape(nc, width))
    ke_ref[...] = (kk * jnp.exp(jnp.broadcast_to(bl3, bh3.shape).reshape(ts, width) - bh)).astype(BF16)
    ki_ref[...] = (kk * jnp.exp(-bh)).astype(BF16)
    q = proj(0, width)
    qd_ref[...] = (q * _sigmoid(q) * (HG_DK ** -0.5) * jnp.exp(bh)).astype(BF16)
    vv_ref[...] = proj(2 * width, 3 * width).astype(BF16)
    g_out = proj(3 * width, 4 * width)
    g_act = g_out * _sigmoid(g_out)

    r_i = lax.broadcasted_iota(jnp.int32, (ts, ts), 0)
    c_i = lax.broadcasted_iota(jnp.int32, (ts, ts), 1)
    causal = (r_i // HG_CHUNK == c_i // HG_CHUNK) & (c_i <= r_i)

    for c in range(nc):
        rows = slice(c * HG_CHUNK, (c + 1) * HG_CHUNK)
        for h in range(HG_HEADS):
            cols = slice(h * HG_DK, (h + 1) * HG_DK)
            st = st_ref[h]
            oi_ref[rows, cols] = _dot_nt(qd_ref[rows, cols], st.astype(BF16))
            st_ref[h] = st * dec_ref[c:c + 1, cols] + _dot_tn(vv_ref[rows, cols], ke_ref[rows, cols])

    for h in range(HG_HEADS):
        cols = slice(h * HG_DK, (h + 1) * HG_DK)
        scores = jnp.where(causal, _dot_nt(qd_ref[:, cols], ki_ref[:, cols]), 0.0)
        o_h = jnp.dot(scores.astype(BF16), vv_ref[:, cols], preferred_element_type=F32) + oi_ref[:, cols]
        o_h = o_h * lax.rsqrt(jnp.mean(o_h * o_h, axis=-1, keepdims=True) + RMS_EPS) * ng_ref[...]
        o_ref[:, cols] = (o_h * g_act[:, cols]).astype(BF16)
    y_a = jnp.dot(o_ref[...], whg_ref[...], preferred_element_type=F32)

    v_pool = proj(4 * width, 4 * width + pool_width)
    ext_ref[MAX_WIN:MAX_WIN + ts, :] = v_pool
    pos = s_idx * ts + lax.broadcasted_iota(jnp.int32, (ts, 1), 0)
    out_group = d // len(POOL_WINDOWS)
    for gi, w in enumerate(POOL_WINDOWS):
        gcols = slice(gi * POOL_GROUP, (gi + 1) * POOL_GROUP)
        wsum = v_pool[:, gcols]
        for j in range(1, w):
            wsum = wsum + ext_ref[MAX_WIN - j:MAX_WIN - j + ts, gcols]
        cnt = jnp.minimum(pos + 1, w).astype(F32)
        pooled = wsum / cnt - v_pool[:, gcols]
        yb_ref[:, gi * out_group:(gi + 1) * out_group] = jnp.dot(
            pooled.astype(BF16), pw_ref[gi], preferred_element_type=F32)
    ext_ref[0:MAX_WIN, :] = ext_ref[ts:ts + MAX_WIN, :]
    y_b = yb_ref[...] * ps_ref[...]

    gate_a = proj(4 * width + pool_width, 4 * width + pool_width + d)
    gate_b = proj(4 * width + pool_width + d, 4 * width + pool_width + 2 * d)
    mix = _sigmoid(gate_a) * y_a + _sigmoid(gate_b) * y_b
    y = alpha * h0 + jnp.dot(mix.astype(BF16), wout_ref[...], preferred_element_type=F32)
    h1_ref[...] = _layer_norm(y, ln1g_ref[...], ln1b_ref[...])


def _const_spec(shape):
    nd = len(shape)
    return pl.BlockSpec(shape, lambda *_: (0,) * nd, pipeline_mode=pl.Buffered(1))


def _mixer(x2, batch, seq, alpha, ln0_g, ln0_b, w_in, hg_lb, norm_g, w_hg, pool_w, pool_scale,
           w_out, ln1_g, ln1_b):
    t, d = x2.shape
    ts = MIXER_TOKENS
    ns = seq // ts
    width = HG_HEADS * HG_DK
    pool_width = len(POOL_WINDOWS) * POOL_GROUP
    row = lambda a: a.reshape(1, -1)
    args = (x2, row(ln0_g), row(ln0_b), w_in.astype(BF16), hg_lb, row(norm_g), w_hg.astype(BF16),
            pool_w.astype(BF16), row(pool_scale), w_out.astype(BF16), row(ln1_g), row(ln1_b))
    in_specs = [pl.BlockSpec((ts, d), lambda b, s: (b * ns + s, 0))]
    in_specs += [_const_spec(a.shape) for a in args[1:]]
    return pl.pallas_call(
        functools.partial(_mixer_kernel, alpha),
        grid=(batch, ns),
        in_specs=in_specs,
        out_specs=pl.BlockSpec((ts, d), lambda b, s: (b * ns + s, 0)),
        out_shape=jax.ShapeDtypeStruct((t, d), F32),
        scratch_shapes=[
            pltpu.VMEM((HG_HEADS, HG_DK, HG_DK), F32),
            pltpu.VMEM((ts + MAX_WIN, pool_width), F32),
            pltpu.VMEM((ts, width), BF16),
            pltpu.VMEM((ts, width), BF16),
            pltpu.VMEM((ts, width), BF16),
            pltpu.VMEM((ts, width), BF16),
            pltpu.VMEM((ts // HG_CHUNK, width), F32),
            pltpu.VMEM((ts, width), BF16),
            pltpu.VMEM((ts, d), F32),
            pltpu.VMEM((ts, width), F32),
        ],
        compiler_params=pltpu.CompilerParams(
            dimension_semantics=("arbitrary", "arbitrary"),
            vmem_limit_bytes=48 * 1024 * 1024),
        name="mixer",
    )(*args)


def _sort_network(n):
    def merge(lo, hi, r):
        step = r * 2
        if step < hi - lo:
            yield from merge(lo, hi, step)
            yield from merge(lo + r, hi, step)
            yield from [(i, i + r) for i in range(lo + r, hi - r, step)]
        else:
            yield (lo, lo + r)

    def sort(lo, hi):
        if hi - lo >= 1:
            mid = lo + (hi - lo) // 2
            yield from sort(lo, mid)
            yield from sort(mid + 1, hi)
            yield from merge(lo, hi, 1)

    return tuple(sort(0, n - 1))


def _bitonic_merge_network(n):
    pairs, dist = [], n // 2
    while dist:
        pairs += [(i, i + dist) for i in range(n) if not i & dist]
        dist //= 2
    return tuple(pairs)


def _ordered(a, b):
    a_first = (a[0] > b[0]) | ((a[0] == b[0]) & (a[1] < b[1]))
    first = (jnp.maximum(a[0], b[0]),) + tuple(jnp.where(a_first, x, y) for x, y in zip(a[1:], b[1:]))
    second = (jnp.minimum(a[0], b[0]),) + tuple(jnp.where(a_first, y, x) for x, y in zip(a[1:], b[1:]))
    return first, second


def _top_k_sorted(elems, k):
    return _merge_sorted_blocks([_apply_network(elems[s:s + k], _sort_network(k))
                                 for s in range(0, len(elems), k)])


def _apply_network(elems, network):
    elems = list(elems)
    for i, j in network:
        elems[i], elems[j] = _ordered(elems[i], elems[j])
    return elems


def _merge_sorted_blocks(blocks):
    k = len(blocks[0])
    while len(blocks) > 1:
        blocks = [_apply_network([_ordered(a[i], b[k - 1 - i])[0] for i in range(k)], _bitonic_merge_network(k))
                  for a, b in zip(blocks[0::2], blocks[1::2])]
    return blocks[0]


def _router_kernel(alpha, h_ref, p_ref, wq_ref, keys_ref, wpg_ref, wpp_ref, ids_ref, gates_ref, base_ref,
                   q_ref, sc_ref, ids_s, gates_s):
    h = h_ref[...]
    hb = h.astype(BF16)
    tb = h.shape[0]
    k = PEER_TOPK
    planes = tb // LANES
    vreg = (planes, LANES)
    ple = _sigmoid(jnp.dot(hb, wpg_ref[...], preferred_element_type=F32)) * jnp.dot(
        p_ref[...].astype(BF16), wpp_ref[...], preferred_element_type=F32)
    base_ref[...] = alpha * h + ple

    for hd in range(PEER_HEADS):
        cols = slice(hd * 2 * PEER_HALF, (hd + 1) * 2 * PEER_HALF)
        q = jnp.dot(hb, wq_ref[:, cols], preferred_element_type=F32).astype(BF16)
        q_ref[2 * hd] = q[:, :PEER_HALF]
        q_ref[2 * hd + 1] = q[:, PEER_HALF:]

    def per_key_rows(ref, row):
        return ref.at[pl.ds(row, planes, stride=LANES), :]

    def head(hd, carry):
        tops = []
        for half in range(2):
            sc = _dot_nt(keys_ref[hd, half], q_ref[2 * hd + half])
            for pln in range(planes):
                sc_ref[pln * LANES:(pln + 1) * LANES, :] = sc[:, pln * LANES:(pln + 1) * LANES]
            elems = [(per_key_rows(sc_ref, key)[...], jnp.full(vreg, float(key), F32))
                     for key in range(PEER_NKEYS)]
            tops.append(_top_k_sorted(elems, k))
        (top1, top2) = tops
        def cand(a, b):
            return (top1[a][0] + top2[b][0], jnp.full(vreg, float(a * k + b), F32),
                    top1[a][1] * PEER_NKEYS + top2[b][1])

        def pad(i):
            return (jnp.full(vreg, -jnp.inf, F32), jnp.full(vreg, float(k * k + i), F32), jnp.zeros(vreg, F32))

        rows = [[cand(a, b) for b in range(k // (a + 1))] for a in range(k)]
        singles = [rows[a][0] for a in range(k // 2, k)]
        mixed = [c for a in range(2, k // 2 - 1) for c in rows[a]]
        last = rows[k // 2 - 1] + [pad(i) for i in range(k - len(rows[k // 2 - 1]))]
        assert len(rows[0]) == k and len(rows[1]) + len(singles) == k and len(mixed) == k
        best = _merge_sorted_blocks([
            rows[0],
            _apply_network(rows[1] + singles[::-1], _bitonic_merge_network(k)),
            _apply_network(mixed, _sort_network(k)),
            last])
        ex = [jnp.exp(e[0] - best[0][0]) for e in best]
        denom = functools.reduce(jnp.add, ex)
        for j in range(k):
            slot = hd * k + j
            per_key_rows(gates_s, slot)[...] = ex[j] / denom
            per_key_rows(ids_s, slot)[...] = best[j][2] * PACK_ROWS
        return carry

    lax.fori_loop(0, PEER_HEADS, head, 0)
    for pln in range(planes):
        rows = slice(pln * LANES, (pln + 1) * LANES)
        gates_ref[rows, :] = gates_s[rows, :].T
        ids_ref[rows, :] = ids_s[rows, :].T.astype(jnp.int32)


def _router(h1, p2, alpha, w_query, sub_keys, w_ple_gate, w_ple_proj):
    t, d = h1.shape
    tb = ROUTER_TOKENS
    slots = PEER_HEADS * PEER_TOPK
    assert slots == LANES and PEER_NKEYS == LANES
    args = (h1, p2, w_query.astype(BF16), sub_keys.astype(BF16), w_ple_gate.astype(BF16),
            w_ple_proj.astype(BF16))
    in_specs = [pl.BlockSpec((tb, d), lambda i: (i, 0)),
                pl.BlockSpec((tb, p2.shape[1]), lambda i: (i, 0))]
    in_specs += [_const_spec(a.shape) for a in args[2:]]
    return pl.pallas_call(
        functools.partial(_router_kernel, alpha),
        grid=(t // tb,),
        in_specs=in_specs,
        out_specs=[pl.BlockSpec((tb, slots), lambda i: (i, 0)),
                   pl.BlockSpec((tb, slots), lambda i: (i, 0)),
                   pl.BlockSpec((tb, d), lambda i: (i, 0))],
        out_shape=[jax.ShapeDtypeStruct((t, slots), jnp.int32),
                   jax.ShapeDtypeStruct((t, slots), F32),
                   jax.ShapeDtypeStruct((t, d), F32)],
        scratch_shapes=[pltpu.VMEM((2 * PEER_HEADS, tb, PEER_HALF), BF16),
                        pltpu.VMEM((tb, LANES), F32),
                        pltpu.VMEM((tb, LANES), F32),
                        pltpu.VMEM((tb, LANES), F32)],
        compiler_params=pltpu.CompilerParams(
            dimension_semantics=("arbitrary",),
            vmem_limit_bytes=56 * 1024 * 1024),
        name="router",
    )(*args)


def _pack_kernel(tab_ref, out_ref):
    half = tab_ref.shape[1] // 2
    words = pltpu.bitcast(pltpu.pack_elementwise([tab_ref[:, :half], tab_ref[:, half:]], packed_dtype=BF16),
                          jnp.uint32)
    rows = tab_ref.shape[0]
    for j in range(PACK_ROWS):
        out_ref[pl.ds(j, rows, stride=PACK_ROWS), :] = words[:, j * LANES:(j + 1) * LANES]


def _pack_table(tab):
    n, d = tab.shape
    rows = PACK_BLOCK_ROWS
    return pl.pallas_call(
        _pack_kernel,
        grid=(n // rows,),
        in_specs=[pl.BlockSpec((rows, d), lambda i: (i, 0))],
        out_specs=pl.BlockSpec((rows * PACK_ROWS, LANES), lambda i: (i, 0)),
        out_shape=jax.ShapeDtypeStruct((n * PACK_ROWS, LANES), jnp.uint32),
        name="pack",
    )(tab)


def _gather_rows(ids_ref, tab_ref, g_ref, t, slots):
    ids_t = ids_ref.at[pl.ds(t * slots, slots)]
    for k in range(slots):
        slab = tab_ref[pl.ds(pl.multiple_of(ids_t[k], PACK_ROWS), PACK_ROWS), :]
        start = (k // SUBLANES) * (PACK_ROWS * SUBLANES) + k % SUBLANES
        g_ref[pl.ds(start, PACK_ROWS, stride=SUBLANES), :] = slab


def _word_tile(g_ref, slots, c):
    return jnp.concatenate(
        [g_ref[(kt * PACK_ROWS + c) * SUBLANES:(kt * PACK_ROWS + c + 1) * SUBLANES, :]
         for kt in range(slots // SUBLANES)], axis=0)


def _gathered_matrix(g_ref, slots):
    words = jnp.concatenate([_word_tile(g_ref, slots, c) for c in range(PACK_ROWS)], axis=1)
    return pltpu.bitcast(words, BF16)


def _pipelined_tokens(tb, gather, compute, bufs):
    g0, g1 = bufs
    gather(0, g0)
    gather(1, g1)

    def pair(i, carry):
        t = 2 * i
        compute(t, g0)
        gather(t + 2, g0)
        compute(t + 1, g1)
        gather(t + 3, g1)
        return carry

    lax.fori_loop(0, tb // 2 - 1, pair, 0)
    compute(tb - 2, g0)
    compute(tb - 1, g1)


def _pair_select(slots):
    r = lax.broadcasted_iota(jnp.int32, (2 * slots, slots), 0)
    c = lax.broadcasted_iota(jnp.int32, (2 * slots, slots), 1)
    return (r // 2 == c).astype(F32)


def _gelu(x):
    return 0.5 * x * (1.0 + lax.erf(x * (2.0 ** -0.5)))


def _split_bf16(x):
    head = x.astype(BF16).astype(F32)
    return head, x - head


def _store_rows(ref, j, tb, value):
    for c in range(ref.shape[0]):
        ref[c, pl.ds(j, tb, stride=SUBLANES), :] = value[:, c * LANES:(c + 1) * LANES]


def _load_rows(ref, j, tb):
    return jnp.concatenate([ref[c, pl.ds(j, tb, stride=SUBLANES), :] for c in range(ref.shape[0])], axis=1)


def _load_group(ref, rows):
    return jnp.concatenate([ref[c, rows, :] for c in range(ref.shape[0])], axis=1)


def _store_group(ref, rows, value):
    for c in range(ref.shape[0]):
        ref[c, rows, :] = value[:, c * LANES:(c + 1) * LANES]


def _peer_u_kernel(ids_ref, h_ref, gate_ref, tab_ref, act_ref, g0_ref, g1_ref, q0_ref, q1_ref, lhs_ref, xq_ref,
                   ra_ref, rq_ref):
    tb, d = h_ref.shape
    slots = gate_ref.shape[1]
    half = d // 2
    parts = 4
    mx = slots // 2

    @pl.when(pl.program_id(0) == 0)
    def _():
        lhs_ref[...] = jnp.zeros_like(lhs_ref)

    x = h_ref[...]
    x_hi, x_lo = _split_bf16(x)
    for j, part in enumerate((x_hi[:, :half], x_lo[:, :half], x_hi[:, half:], x_lo[:, half:])):
        _store_rows(lhs_ref, j, tb, part)
    for r in range(SUBLANES):
        c = (r // 2) + (r % 2) * PACK_ROWS
        xq_ref[pl.ds(r, tb, stride=SUBLANES), :] = x[:, c * LANES:(c + 1) * LANES]

    def gather(t, bufs):
        g_ref, q_ref = bufs
        ids_t = ids_ref.at[pl.ds(t * slots, slots)]
        xq = xq_ref[pl.ds(pl.multiple_of(t * SUBLANES, SUBLANES), SUBLANES), :]
        for k in range(slots):
            slab = tab_ref[pl.ds(pl.multiple_of(ids_t[k], PACK_ROWS), PACK_ROWS), :]
            if k < mx:
                start = (k // SUBLANES) * (PACK_ROWS * SUBLANES) + k % SUBLANES
                g_ref[pl.ds(start, PACK_ROWS, stride=SUBLANES), :] = slab
            else:
                prod = pltpu.bitcast(slab, BF16).astype(F32) * xq
                q_ref[k - mx:k - mx + 1, :] = jnp.sum(prod, axis=0, keepdims=True)

    ones = jnp.ones((SUBLANES, LANES), BF16)

    def compute(t, bufs):
        g_ref, q_ref = bufs
        rows = pl.ds(pl.multiple_of(t * SUBLANES, SUBLANES), SUBLANES)
        ra_ref[rows, :] = _dot_nt(_load_group(lhs_ref, rows).astype(BF16), _gathered_matrix(g_ref, mx))
        q_hi, q_lo = _split_bf16(q_ref[...])
        rq_ref[rows, 0:mx] = _dot_nt(ones, q_hi.astype(BF16)) + _dot_nt(ones, q_lo.astype(BF16))

    _pipelined_tokens(tb, gather, compute, ((g0_ref, q0_ref), (g1_ref, q1_ref)))

    r = [ra_ref[pl.ds(j, tb, stride=SUBLANES), :] for j in range(parts)]
    lane_even = lax.broadcasted_iota(jnp.int32, (1, 2 * mx), 1) % 2 == 0
    inter = jnp.where(lane_even, r[0] + r[1], r[2] + r[3])
    pre_a = jnp.dot(inter, _pair_select(mx), preferred_element_type=F32, precision=lax.Precision.HIGHEST)
    pre_b = rq_ref[pl.ds(0, tb, stride=SUBLANES), :][:, 0:mx]
    act_ref[...] = _gelu(jnp.concatenate([pre_a, pre_b], axis=1)) * gate_ref[...]


def _peer_v_kernel(ids_ref, act_ref, base_ref, tab_ref, lng_ref, lnb_ref, out_ref, g0_ref, g1_ref, lhs_ref,
                   r_ref):
    tb, d = base_ref.shape
    slots = act_ref.shape[1]

    @pl.when(pl.program_id(0) == 0)
    def _():
        lhs_ref[...] = jnp.zeros_like(lhs_ref)

    a2 = lax.dot_general(act_ref[...], _pair_select(slots), (((1,), (1,)), ((), ())),
                         preferred_element_type=F32, precision=lax.Precision.HIGHEST)
    lane_even = lax.broadcasted_iota(jnp.int32, (1, 2 * slots), 1) % 2 == 0
    a_hi, a_lo = _split_bf16(a2)
    for j, part in enumerate((jnp.where(lane_even, a_hi, 0.0), jnp.where(lane_even, a_lo, 0.0),
                              jnp.where(lane_even, 0.0, a_hi), jnp.where(lane_even, 0.0, a_lo))):
        _store_rows(lhs_ref, j, tb, part)

    def gather(t, g_ref):
        _gather_rows(ids_ref, tab_ref, g_ref, t, slots)

    def compute(t, g_ref):
        rows = pl.ds(pl.multiple_of(t * SUBLANES, SUBLANES), SUBLANES)
        _store_group(r_ref, rows, jnp.dot(_load_group(lhs_ref, rows).astype(BF16),
                                          _gathered_matrix(g_ref, slots), preferred_element_type=F32))

    _pipelined_tokens(tb, gather, compute, (g0_ref, g1_ref))

    r = [_load_rows(r_ref, j, tb) for j in range(4)]
    ffn = jnp.concatenate([r[0] + r[1], r[2] + r[3]], axis=1)
    out_ref[...] = _layer_norm(base_ref[...] + ffn, lng_ref[...], lnb_ref[...])


def _peer_specs(tb, d, slots, tab):
    ids_spec = pl.BlockSpec((tb * slots,), lambda i: (i,), memory_space=pltpu.SMEM)
    tile_spec = pl.BlockSpec((tb, d), lambda i: (i, 0))
    slot_spec = pl.BlockSpec((tb, slots), lambda i: (i, 0))
    return ids_spec, tile_spec, slot_spec, _const_spec(tab.shape)


def _peer_u(ids_flat, h1, gates, tab):
    t, d = h1.shape
    slots = gates.shape[1]
    tb = PEER_TOKENS
    ids_spec, tile_spec, slot_spec, tab_spec = _peer_specs(tb, d, slots, tab)
    return pl.pallas_call(
        _peer_u_kernel,
        grid=(t // tb,),
        in_specs=[ids_spec, tile_spec, slot_spec, tab_spec],
        out_specs=slot_spec,
        out_shape=jax.ShapeDtypeStruct((t, slots), F32),
        scratch_shapes=[
            *[pltpu.VMEM((slots // 2 * PACK_ROWS, LANES), jnp.uint32)] * 2,
            *[pltpu.VMEM((slots // 2, LANES), F32)] * 2,
            pltpu.VMEM((d // 2 // LANES, tb * SUBLANES, LANES), F32),
            pltpu.VMEM((tb * SUBLANES, LANES), F32),
            pltpu.VMEM((tb * SUBLANES, slots), F32),
            pltpu.VMEM((tb * SUBLANES, LANES), F32),
        ],
        compiler_params=pltpu.CompilerParams(
            dimension_semantics=("arbitrary",),
            vmem_limit_bytes=52 * 1024 * 1024),
        name="peer_u",
    )(ids_flat, h1, gates, tab)


def _peer_v(ids_flat, act, base, tab, ln_g, ln_b):
    t, d = base.shape
    slots = act.shape[1]
    tb = PEER_TOKENS
    ids_spec, tile_spec, slot_spec, tab_spec = _peer_specs(tb, d, slots, tab)
    row = lambda a: a.reshape(1, -1)
    return pl.pallas_call(
        _peer_v_kernel,
        grid=(t // tb,),
        in_specs=[ids_spec, slot_spec, tile_spec, tab_spec, _const_spec((1, d)), _const_spec((1, d))],
        out_specs=tile_spec,
        out_shape=jax.ShapeDtypeStruct((t, d), F32),
        scratch_shapes=[
            *[pltpu.VMEM((slots * PACK_ROWS, LANES), jnp.uint32)] * 2,
            pltpu.VMEM((2 * slots // LANES, tb * SUBLANES, LANES), F32),
            pltpu.VMEM((d // 2 // LANES, tb * SUBLANES, LANES), F32),
        ],
        compiler_params=pltpu.CompilerParams(
            dimension_semantics=("arbitrary",),
            vmem_limit_bytes=52 * 1024 * 1024),
        name="peer_v",
    )(ids_flat, act, base, tab, row(ln_g), row(ln_b))


def kernel(x, p, ln0_g, ln0_b, w_in, hg_lb, hg_norm_g, w_hg_branch, pool_w, pool_scale, w_out,
           ln1_g, ln1_b, w_query, sub_keys, u_tab, v_tab, w_ple_gate, w_ple_proj, ln2_g, ln2_b):
    batch, seq, d = x.shape
    depth = w_in.shape[0]
    t = batch * seq
    assert depth == 1 and seq % MIXER_TOKENS == 0 and t % ROUTER_TOKENS == 0 and t % PEER_TOKENS == 0
    alpha = (2.0 * depth) ** 0.25
    h1 = _mixer(x.reshape(t, d), batch, seq, alpha, ln0_g, ln0_b, w_in[0], hg_lb, hg_norm_g[0],
                w_hg_branch[0], pool_w[0], pool_scale[0], w_out[0], ln1_g[0], ln1_b[0])
    ids, gates, base = _router(h1, p[0].reshape(t, -1), alpha, w_query[0], sub_keys[0],
                               w_ple_gate[0], w_ple_proj[0])
    ids_flat = ids.reshape(-1)
    act = _peer_u(ids_flat, h1, gates, _pack_table(u_tab[0]))
    out = _peer_v(ids_flat, act, base, _pack_table(v_tab[0]), ln2_g[0], ln2_b[0])
    return out.reshape(batch, seq, d)
```

```python
import functools

import jax
import jax.numpy as jnp
from jax import lax
from jax.experimental import pallas as pl
from jax.experimental.pallas import tpu as pltpu

F32 = jnp.float32
BF16 = jnp.bfloat16

HG_HEADS = 8
HG_DK = 128
HG_CHUNK = 32
POOL_WINDOWS = (2, 4, 8, 16)
POOL_GROUP = 128
MAX_WIN = 16
PEER_HEADS = 8
PEER_NKEYS = 128
PEER_HALF = 128
PEER_TOPK = 16
LN_EPS = 1e-5
RMS_EPS = 1e-6

LANES = 128
SUBLANES = 8

MIXER_TOKENS = 256
ROUTER_TOKENS = 1024
PEER_TOKENS = 256
PACK_ROWS = 4
PACK_BLOCK_ROWS = 512


def _layer_norm(x, g, b):
    mu = jnp.mean(x, axis=-1, keepdims=True)
    xc = x - mu
    var = jnp.mean(xc * xc, axis=-1, keepdims=True)
    return xc * lax.rsqrt(var + LN_EPS) * g + b


def _sigmoid(x):
    return 1.0 / (1.0 + jnp.exp(-x))


def _dot_nt(a, b):
    return lax.dot_general(a, b, (((1,), (1,)), ((), ())), preferred_element_type=F32)


def _dot_tn(a, b):
    return lax.dot_general(a, b, (((0,), (0,)), ((), ())), preferred_element_type=F32)


def _mixer_kernel(alpha, x_ref, ln0g_ref, ln0b_ref, win_ref, hglb_ref, ng_ref, whg_ref, pw_ref,
                  ps_ref, wout_ref, ln1g_ref, ln1b_ref, h1_ref,
                  st_ref, ext_ref, qd_ref, ki_ref, ke_ref, vv_ref, dec_ref, o_ref, yb_ref, oi_ref):
    ts, d = x_ref.shape
    nc = ts // HG_CHUNK
    width = HG_HEADS * HG_DK
    pool_width = len(POOL_WINDOWS) * POOL_GROUP
    s_idx = pl.program_id(1)

    @pl.when(s_idx == 0)
    def _():
        st_ref[...] = jnp.zeros_like(st_ref)
        ext_ref[0:MAX_WIN, :] = jnp.zeros((MAX_WIN, pool_width), F32)

    h0 = _layer_norm(x_ref[...], ln0g_ref[...], ln0b_ref[...])
    hb = h0.astype(BF16)

    def proj(lo, hi):
        return jnp.dot(hb, win_ref[:, lo:hi], preferred_element_type=F32)

    lbl = hglb_ref[...]
    lmax = jnp.max(lbl, axis=0, keepdims=True)
    lexp = jnp.exp(lbl - lmax)
    lb = lexp[0:1, :] / jnp.sum(lexp, axis=0, keepdims=True)

    f = lb + (1.0 - lb) * _sigmoid(proj(width, 2 * width))
    kk = 1.0 - f
    bh = jnp.log(f)
    row_in_chunk = lax.broadcasted_iota(jnp.int32, (ts, 1), 0) % HG_CHUNK
    sh = 1
    while sh < HG_CHUNK:
        bh = bh + jnp.where(row_in_chunk >= sh, pltpu.roll(bh, sh, axis=0), 0.0)
        sh *= 2
    bh3 = bh.reshape(nc, HG_CHUNK, width)
    bl3 = bh3[:, HG_CHUNK - 1:HG_CHUNK, :]
    dec_ref[...] = jnp.exp(bl3.reshape(nc, width))
    ke_ref[...] = (kk * jnp.exp(jnp.broadcast_to(bl3, bh3.shape).reshape(ts, width) - bh)).astype(BF16)
    ki_ref[...] = (kk * jnp.exp(-bh)).astype(BF16)
    q = proj(0, width)
    qd_ref[...] = (q * _sigmoid(q) * (HG_DK ** -0.5) * jnp.exp(bh)).astype(BF16)
    vv_ref[...] = proj(2 * width, 3 * width).astype(BF16)
    g_out = proj(3 * width, 4 * width)
    g_act = g_out * _sigmoid(g_out)

    r_i = lax.broadcasted_iota(jnp.int32, (ts, ts), 0)
    c_i = lax.broadcasted_iota(jnp.int32, (ts, ts), 1)
    causal = (r_i // HG_CHUNK == c_i // HG_CHUNK) & (c_i <= r_i)

    for c in range(nc):
        rows = slice(c * HG_CHUNK, (c + 1) * HG_CHUNK)
        for h in range(HG_HEADS):
            cols = slice(h * HG_DK, (h + 1) * HG_DK)
            st = st_ref[h]
            oi_ref[rows, cols] = _dot_nt(qd_ref[rows, cols], st.astype(BF16))
            st_ref[h] = st * dec_ref[c:c + 1, cols] + _dot_tn(vv_ref[rows, cols], ke_ref[rows, cols])

    for h in range(HG_HEADS):
        cols = slice(h * HG_DK, (h + 1) * HG_DK)
        scores = jnp.where(causal, _dot_nt(qd_ref[:, cols], ki_ref[:, cols]), 0.0)
        o_h = jnp.dot(scores.astype(BF16), vv_ref[:, cols], preferred_element_type=F32) + oi_ref[:, cols]
        o_h = o_h * lax.rsqrt(jnp.mean(o_h * o_h, axis=-1, keepdims=True) + RMS_EPS) * ng_ref[...]
        o_ref[:, cols] = (o_h * g_act[:, cols]).astype(BF16)
    y_a = jnp.dot(o_ref[...], whg_ref[...], preferred_element_type=F32)

    v_pool = proj(4 * width, 4 * width + pool_width)
    ext_ref[MAX_WIN:MAX_WIN + ts, :] = v_pool
    pos = s_idx * ts + lax.broadcasted_iota(jnp.int32, (ts, 1), 0)
    out_group = d // len(POOL_WINDOWS)
    for gi, w in enumerate(POOL_WINDOWS):
        gcols = slice(gi * POOL_GROUP, (gi + 1) * POOL_GROUP)
        acc = ext_ref[:, gcols]
        span = 1
        while span < w:
            acc = acc + pltpu.roll(acc, span, axis=0)
            span *= 2
        wsum = acc[MAX_WIN:, :]
        cnt = jnp.minimum(pos + 1, w).astype(F32)
        pooled = wsum / cnt - v_pool[:, gcols]
        yb_ref[:, gi * out_group:(gi + 1) * out_group] = jnp.dot(
            pooled.astype(BF16), pw_ref[gi], preferred_element_type=F32)
    ext_ref[0:MAX_WIN, :] = ext_ref[ts:ts + MAX_WIN, :]
    y_b = yb_ref[...] * ps_ref[...]

    gate_a = proj(4 * width + pool_width, 4 * width + pool_width + d)
    gate_b = proj(4 * width + pool_width + d, 4 * width + pool_width + 2 * d)
    mix = _sigmoid(gate_a) * y_a + _sigmoid(gate_b) * y_b
    y = alpha * h0 + jnp.dot(mix.astype(BF16), wout_ref[...], preferred_element_type=F32)
    h1_ref[...] = _layer_norm(y, ln1g_ref[...], ln1b_ref[...])


def _const_spec(shape):
    nd = len(shape)
    return pl.BlockSpec(shape, lambda *_: (0,) * nd, pipeline_mode=pl.Buffered(1))


def _mixer(x2, batch, seq, alpha, ln0_g, ln0_b, w_in, hg_lb, norm_g, w_hg, pool_w, pool_scale,
           w_out, ln1_g, ln1_b):
    t, d = x2.shape
    ts = MIXER_TOKENS
    ns = seq // ts
    width = HG_HEADS * HG_DK
    pool_width = len(POOL_WINDOWS) * POOL_GROUP
    row = lambda a: a.reshape(1, -1)
    args = (x2, row(ln0_g), row(ln0_b), w_in.astype(BF16), hg_lb, row(norm_g), w_hg.astype(BF16),
            pool_w.astype(BF16), row(pool_scale), w_out.astype(BF16), row(ln1_g), row(ln1_b))
    in_specs = [pl.BlockSpec((ts, d), lambda b, s: (b * ns + s, 0))]
    in_specs += [_const_spec(a.shape) for a in args[1:]]
    return pl.pallas_call(
        functools.partial(_mixer_kernel, alpha),
        grid=(batch, ns),
        in_specs=in_specs,
        out_specs=pl.BlockSpec((ts, d), lambda b, s: (b * ns + s, 0)),
        out_shape=jax.ShapeDtypeStruct((t, d), F32),
        scratch_shapes=[
            pltpu.VMEM((HG_HEADS, HG_DK, HG_DK), F32),
            pltpu.VMEM((ts + MAX_WIN, pool_width), F32),
            pltpu.VMEM((ts, width), BF16),
            pltpu.VMEM((ts, width), BF16),
            pltpu.VMEM((ts, width), BF16),
            pltpu.VMEM((ts, width), BF16),
            pltpu.VMEM((ts // HG_CHUNK, width), F32),
            pltpu.VMEM((ts, width), BF16),
            pltpu.VMEM((ts, d), F32),
            pltpu.VMEM((ts, width), F32),
        ],
        compiler_params=pltpu.CompilerParams(
            dimension_semantics=("arbitrary", "arbitrary"),
            vmem_limit_bytes=48 * 1024 * 1024),
        name="mixer",
    )(*args)


def _sort_network(n):
    def merge(lo, hi, r):
        step = r * 2
        if step < hi - lo:
            yield from merge(lo, hi, step)
            yield from merge(lo + r, hi, step)
            yield from [(i, i + r) for i in range(lo + r, hi - r, step)]
        else:
            yield (lo, lo + r)

    def sort(lo, hi):
        if hi - lo >= 1:
            mid = lo + (hi - lo) // 2
            yield from sort(lo, mid)
            yield from sort(mid + 1, hi)
            yield from merge(lo, hi, 1)

    return tuple(sort(0, n - 1))


def _bitonic_merge_network(n):
    pairs, dist = [], n // 2
    while dist:
        pairs += [(i, i + dist) for i in range(n) if not i & dist]
        dist //= 2
    return tuple(pairs)


def _ordered(a, b):
    a_first = (a[0] > b[0]) | ((a[0] == b[0]) & (a[1] < b[1]))
    first = (jnp.maximum(a[0], b[0]),) + tuple(jnp.where(a_first, x, y) for x, y in zip(a[1:], b[1:]))
    second = (jnp.minimum(a[0], b[0]),) + tuple(jnp.where(a_first, y, x) for x, y in zip(a[1:], b[1:]))
    return first, second


def _top_k_sorted(elems, k):
    return _merge_sorted_blocks([_apply_network(elems[s:s + k], _sort_network(k))
                                 for s in range(0, len(elems), k)])


def _apply_network(elems, network):
    elems = list(elems)
    for i, j in network:
        elems[i], elems[j] = _ordered(elems[i], elems[j])
    return elems


def _merge_sorted_blocks(blocks):
    k = len(blocks[0])
    while len(blocks) > 1:
        blocks = [_apply_network([_ordered(a[i], b[k - 1 - i])[0] for i in range(k)], _bitonic_merge_network(k))
                  for a, b in zip(blocks[0::2], blocks[1::2])]
    return blocks[0]


def _router_kernel(alpha, h_ref, p_ref, wq_ref, keys_ref, wpg_ref, wpp_ref, ids_ref, gates_ref, base_ref,
                   q_ref, sc_ref, ids_s, gates_s):
    h = h_ref[...]
    hb = h.astype(BF16)
    tb = h.shape[0]
    k = PEER_TOPK
    planes = tb // LANES
    vreg = (planes, LANES)
    ple = _sigmoid(jnp.dot(hb, wpg_ref[...], preferred_element_type=F32)) * jnp.dot(
        p_ref[...].astype(BF16), wpp_ref[...], preferred_element_type=F32)
    base_ref[...] = alpha * h + ple

    for hd in range(PEER_HEADS):
        cols = slice(hd * 2 * PEER_HALF, (hd + 1) * 2 * PEER_HALF)
        q = jnp.dot(hb, wq_ref[:, cols], preferred_element_type=F32).astype(BF16)
        q_ref[2 * hd] = q[:, :PEER_HALF]
        q_ref[2 * hd + 1] = q[:, PEER_HALF:]

    def per_key_rows(ref, row):
        return ref.at[pl.ds(row, planes, stride=LANES), :]

    def head(hd, carry):
        tops = []
        for half in range(2):
            sc = _dot_nt(keys_ref[hd, half], q_ref[2 * hd + half])
            for pln in range(planes):
                sc_ref[pln * LANES:(pln + 1) * LANES, :] = sc[:, pln * LANES:(pln + 1) * LANES]
            elems = [(per_key_rows(sc_ref, key)[...], jnp.full(vreg, float(key), F32))
                     for key in range(PEER_NKEYS)]
            tops.append(_top_k_sorted(elems, k))
        (top1, top2) = tops
        def cand(a, b):
            return (top1[a][0] + top2[b][0], jnp.full(vreg, float(a * k + b), F32),
                    top1[a][1] * PEER_NKEYS + top2[b][1])

        def pad(i):
            return (jnp.full(vreg, -jnp.inf, F32), jnp.full(vreg, float(k * k + i), F32), jnp.zeros(vreg, F32))

        rows = [[cand(a, b) for b in range(k // (a + 1))] for a in range(k)]
        singles = [rows[a][0] for a in range(k // 2, k)]
        mixed = [c for a in range(2, k // 2 - 1) for c in rows[a]]
        last = rows[k // 2 - 1] + [pad(i) for i in range(k - len(rows[k // 2 - 1]))]
        assert len(rows[0]) == k and len(rows[1]) + len(singles) == k and len(mixed) == k
        best = _merge_sorted_blocks([
            rows[0],
            _apply_network(rows[1] + singles[::-1], _bitonic_merge_network(k)),
            _apply_network(mixed, _sort_network(k)),
            last])
        ex = [jnp.exp(e[0] - best[0][0]) for e in best]
        denom = functools.reduce(jnp.add, ex)
        for j in range(k):
            slot = hd * k + j
            per_key_rows(gates_s, slot)[...] = ex[j] / denom
            per_key_rows(ids_s, slot)[...] = best[j][2] * PACK_ROWS
        return carry

    lax.fori_loop(0, PEER_HEADS, head, 0)
    for pln in range(planes):
        rows = slice(pln * LANES, (pln + 1) * LANES)
        gates_ref[rows, :] = gates_s[rows, :].T
        ids_ref[rows, :] = ids_s[rows, :].T.astype(jnp.int32)


def _router(h1, p2, alpha, w_query, sub_keys, w_ple_gate, w_ple_proj):
    t, d = h1.shape
    tb = ROUTER_TOKENS
    slots = PEER_HEADS * PEER_TOPK
    assert slots == LANES and PEER_NKEYS == LANES
    args = (h1, p2, w_query.astype(BF16), sub_keys.astype(BF16), w_ple_gate.astype(BF16),
            w_ple_proj.astype(BF16))
    in_specs = [pl.BlockSpec((tb, d), lambda i: (i, 0)),
                pl.BlockSpec((tb, p2.shape[1]), lambda i: (i, 0))]
    in_specs += [_const_spec(a.shape) for a in args[2:]]
    return pl.pallas_call(
        functools.partial(_router_kernel, alpha),
        grid=(t // tb,),
        in_specs=in_specs,
        out_specs=[pl.BlockSpec((tb, slots), lambda i: (i, 0)),
                   pl.BlockSpec((tb, slots), lambda i: (i, 0)),
                   pl.BlockSpec((tb, d), lambda i: (i, 0))],
        out_shape=[jax.ShapeDtypeStruct((t, slots), jnp.int32),
                   jax.ShapeDtypeStruct((t, slots), F32),
                   jax.ShapeDtypeStruct((t, d), F32)],
        scratch_shapes=[pltpu.VMEM((2 * PEER_HEADS, tb, PEER_HALF), BF16),
                        pltpu.VMEM((tb, LANES), F32),
                        pltpu.VMEM((tb, LANES), F32),
                        pltpu.VMEM((tb, LANES), F32)],
        compiler_params=pltpu.CompilerParams(
            dimension_semantics=("arbitrary",),
            vmem_limit_bytes=56 * 1024 * 1024),
        name="router",
    )(*args)


def _pack_kernel(tab_ref, out_ref):
    half = tab_ref.shape[1] // 2
    words = pltpu.bitcast(pltpu.pack_elementwise([tab_ref[:, :half], tab_ref[:, half:]], packed_dtype=BF16),
                          jnp.uint32)
    rows = tab_ref.shape[0]
    for j in range(PACK_ROWS):
        out_ref[pl.ds(j, rows, stride=PACK_ROWS), :] = words[:, j * LANES:(j + 1) * LANES]


def _pack_table(tab):
    n, d = tab.shape
    rows = PACK_BLOCK_ROWS
    return pl.pallas_call(
        _pack_kernel,
        grid=(n // rows,),
        in_specs=[pl.BlockSpec((rows, d), lambda i: (i, 0))],
        out_specs=pl.BlockSpec((rows * PACK_ROWS, LANES), lambda i: (i, 0)),
        out_shape=jax.ShapeDtypeStruct((n * PACK_ROWS, LANES), jnp.uint32),
        name="pack",
    )(tab)


def _gather_rows(ids_ref, tab_ref, g_ref, t, slots):
    ids_t = ids_ref.at[pl.ds(t * slots, slots)]
    for k in range(slots):
        slab = tab_ref[pl.ds(pl.multiple_of(ids_t[k], PACK_ROWS), PACK_ROWS), :]
        start = (k // SUBLANES) * (PACK_ROWS * SUBLANES) + k % SUBLANES
        g_ref[pl.ds(start, PACK_ROWS, stride=SUBLANES), :] = slab


def _word_tile(g_ref, slots, c):
    return jnp.concatenate(
        [g_ref[(kt * PACK_ROWS + c) * SUBLANES:(kt * PACK_ROWS + c + 1) * SUBLANES, :]
         for kt in range(slots // SUBLANES)], axis=0)


def _gathered_matrix(g_ref, slots):
    words = jnp.concatenate([_word_tile(g_ref, slots, c) for c in range(PACK_ROWS)], axis=1)
    return pltpu.bitcast(words, BF16)


def _pipelined_tokens(tb, gather, compute, bufs):
    g0, g1 = bufs
    gather(0, g0)
    gather(1, g1)

    def pair(i, carry):
        t = 2 * i
        compute(t, g0)
        gather(t + 2, g0)
        compute(t + 1, g1)
        gather(t + 3, g1)
        return carry

    lax.fori_loop(0, tb // 2 - 1, pair, 0)
    compute(tb - 2, g0)
    compute(tb - 1, g1)


def _pair_select(slots):
    r = lax.broadcasted_iota(jnp.int32, (2 * slots, slots), 0)
    c = lax.broadcasted_iota(jnp.int32, (2 * slots, slots), 1)
    return (r // 2 == c).astype(F32)


def _gelu(x):
    return 0.5 * x * (1.0 + lax.erf(x * (2.0 ** -0.5)))


def _split_bf16(x):
    head = x.astype(BF16).astype(F32)
    return head, x - head


def _store_rows(ref, j, tb, value):
    for c in range(ref.shape[0]):
        ref[c, pl.ds(j, tb, stride=SUBLANES), :] = value[:, c * LANES:(c + 1) * LANES]


def _load_rows(ref, j, tb):
    return jnp.concatenate([ref[c, pl.ds(j, tb, stride=SUBLANES), :] for c in range(ref.shape[0])], axis=1)


def _load_group(ref, rows):
    return jnp.concatenate([ref[c, rows, :] for c in range(ref.shape[0])], axis=1)


def _store_group(ref, rows, value):
    for c in range(ref.shape[0]):
        ref[c, rows, :] = value[:, c * LANES:(c + 1) * LANES]


def _peer_u_kernel(ids_ref, h_ref, gate_ref, tab_ref, act_ref, g0_ref, g1_ref, q0_ref, q1_ref, lhs_ref, xq_ref,
                   ra_ref, rq_ref):
    tb, d = h_ref.shape
    slots = gate_ref.shape[1]
    half = d // 2
    parts = 4
    mx = slots // 2

    @pl.when(pl.program_id(0) == 0)
    def _():
        lhs_ref[...] = jnp.zeros_like(lhs_ref)

    x = h_ref[...]
    x_hi, x_lo = _split_bf16(x)
    for j, part in enumerate((x_hi[:, :half], x_lo[:, :half], x_hi[:, half:], x_lo[:, half:])):
        _store_rows(lhs_ref, j, tb, part)
    for r in range(SUBLANES):
        c = (r // 2) + (r % 2) * PACK_ROWS
        xq_ref[pl.ds(r, tb, stride=SUBLANES), :] = x[:, c * LANES:(c + 1) * LANES]

    def gather(t, bufs):
        g_ref, q_ref = bufs
        ids_t = ids_ref.at[pl.ds(t * slots, slots)]
        xq = xq_ref[pl.ds(pl.multiple_of(t * SUBLANES, SUBLANES), SUBLANES), :]
        for k in range(slots):
            slab = tab_ref[pl.ds(pl.multiple_of(ids_t[k], PACK_ROWS), PACK_ROWS), :]
            if k < mx:
                start = (k // SUBLANES) * (PACK_ROWS * SUBLANES) + k % SUBLANES
                g_ref[pl.ds(start, PACK_ROWS, stride=SUBLANES), :] = slab
            else:
                prod = pltpu.bitcast(slab, BF16).astype(F32) * xq
                q_ref[k - mx:k - mx + 1, :] = jnp.sum(prod, axis=0, keepdims=True)

    ones = jnp.ones((SUBLANES, LANES), BF16)

    def compute(t, bufs):
        g_ref, q_ref = bufs
        rows = pl.ds(pl.multiple_of(t * SUBLANES, SUBLANES), SUBLANES)
        ra_ref[rows, :] = _dot_nt(_load_group(lhs_ref, rows).astype(BF16), _gathered_matrix(g_ref, mx))
        q_hi, q_lo = _split_bf16(q_ref[...])
        rq_ref[rows, 0:mx] = _dot_nt(ones, q_hi.astype(BF16)) + _dot_nt(ones, q_lo.astype(BF16))

    _pipelined_tokens(tb, gather, compute, ((g0_ref, q0_ref), (g1_ref, q1_ref)))

    r = [ra_ref[pl.ds(j, tb, stride=SUBLANES), :] for j in range(parts)]
    lane_even = lax.broadcasted_iota(jnp.int32, (1, 2 * mx), 1) % 2 == 0
    inter = jnp.where(lane_even, r[0] + r[1], r[2] + r[3])
    pre_a = jnp.dot(inter, _pair_select(mx), preferred_element_type=F32, precision=lax.Precision.HIGHEST)
    pre_b = rq_ref[pl.ds(0, tb, stride=SUBLANES), :][:, 0:mx]
    act_ref[...] = _gelu(jnp.concatenate([pre_a, pre_b], axis=1)) * gate_ref[...]


def _peer_v_kernel(ids_ref, act_ref, base_ref, tab_ref, lng_ref, lnb_ref, out_ref, g0_ref, g1_ref, lhs_ref,
                   r_ref):
    tb, d = base_ref.shape
    slots = act_ref.shape[1]

    @pl.when(pl.program_id(0) == 0)
    def _():
        lhs_ref[...] = jnp.zeros_like(lhs_ref)

    a2 = lax.dot_general(act_ref[...], _pair_select(slots), (((1,), (1,)), ((), ())),
                         preferred_element_type=F32, precision=lax.Precision.HIGHEST)
    lane_even = lax.broadcasted_iota(jnp.int32, (1, 2 * slots), 1) % 2 == 0
    a_hi, a_lo = _split_bf16(a2)
    for j, part in enumerate((jnp.where(lane_even, a_hi, 0.0), jnp.where(lane_even, a_lo, 0.0),
                              jnp.where(lane_even, 0.0, a_hi), jnp.where(lane_even, 0.0, a_lo))):
        _store_rows(lhs_ref, j, tb, part)

    def gather(t, g_ref):
        _gather_rows(ids_ref, tab_ref, g_ref, t, slots)

    def compute(t, g_ref):
        rows = pl.ds(pl.multiple_of(t * SUBLANES, SUBLANES), SUBLANES)
        _store_group(r_ref, rows, jnp.dot(_load_group(lhs_ref, rows).astype(BF16),
                                          _gathered_matrix(g_ref, slots), preferred_element_type=F32))

    _pipelined_tokens(tb, gather, compute, (g0_ref, g1_ref))

    r = [_load_rows(r_ref, j, tb) for j in range(4)]
    ffn = jnp.concatenate([r[0] + r[1], r[2] + r[3]], axis=1)
    out_ref[...] = _layer_norm(base_ref[...] + ffn, lng_ref[...], lnb_ref[...])


def _peer_specs(tb, d, slots, tab):
    ids_spec = pl.BlockSpec((tb * slots,), lambda i: (i,), memory_space=pltpu.SMEM)
    tile_spec = pl.BlockSpec((tb, d), lambda i: (i, 0))
    slot_spec = pl.BlockSpec((tb, slots), lambda i: (i, 0))
    return ids_spec, tile_spec, slot_spec, _const_spec(tab.shape)


def _peer_u(ids_flat, h1, gates, tab):
    t, d = h1.shape
    slots = gates.shape[1]
    tb = PEER_TOKENS
    ids_spec, tile_spec, slot_spec, tab_spec = _peer_specs(tb, d, slots, tab)
    return pl.pallas_call(
        _peer_u_kernel,
        grid=(t // tb,),
        in_specs=[ids_spec, tile_spec, slot_spec, tab_spec],
        out_specs=slot_spec,
        out_shape=jax.ShapeDtypeStruct((t, slots), F32),
        scratch_shapes=[
            *[pltpu.VMEM((slots // 2 * PACK_ROWS, LANES), jnp.uint32)] * 2,
            *[pltpu.VMEM((slots // 2, LANES), F32)] * 2,
            pltpu.VMEM((d // 2 // LANES, tb * SUBLANES, LANES), F32),
            pltpu.VMEM((tb * SUBLANES, LANES), F32),
            pltpu.VMEM((tb * SUBLANES, slots), F32),
            pltpu.VMEM((tb * SUBLANES, LANES), F32),
        ],
        compiler_params=pltpu.CompilerParams(
            dimension_semantics=("arbitrary",),
            vmem_limit_bytes=52 * 1024 * 1024),
        name="peer_u",
    )(ids_flat, h1, gates, tab)


def _peer_v(ids_flat, act, base, tab, ln_g, ln_b):
    t, d = base.shape
    slots = act.shape[1]
    tb = PEER_TOKENS
    ids_spec, tile_spec, slot_spec, tab_spec = _peer_specs(tb, d, slots, tab)
    row = lambda a: a.reshape(1, -1)
    return pl.pallas_call(
        _peer_v_kernel,
        grid=(t // tb,),
        in_specs=[ids_spec, slot_spec, tile_spec, tab_spec, _const_spec((1, d)), _const_spec((1, d))],
        out_specs=tile_spec,
        out_shape=jax.ShapeDtypeStruct((t, d), F32),
        scratch_shapes=[
            *[pltpu.VMEM((slots * PACK_ROWS, LANES), jnp.uint32)] * 2,
            pltpu.VMEM((2 * slots // LANES, tb * SUBLANES, LANES), F32),
            pltpu.VMEM((d // 2 // LANES, tb * SUBLANES, LANES), F32),
        ],
        compiler_params=pltpu.CompilerParams(
            dimension_semantics=("arbitrary",),
            vmem_limit_bytes=52 * 1024 * 1024),
        name="peer_v",
    )(ids_flat, act, base, tab, row(ln_g), row(ln_b))


def kernel(x, p, ln0_g, ln0_b, w_in, hg_lb, hg_norm_g, w_hg_branch, pool_w, pool_scale, w_out,
           ln1_g, ln1_b, w_query, sub_keys, u_tab, v_tab, w_ple_gate, w_ple_proj, ln2_g, ln2_b):
    batch, seq, d = x.shape
    depth = w_in.shape[0]
    t = batch * seq
    assert depth == 1 and seq % MIXER_TOKENS == 0 and t % ROUTER_TOKENS == 0 and t % PEER_TOKENS == 0
    alpha = (2.0 * depth) ** 0.25
    h1 = _mixer(x.reshape(t, d), batch, seq, alpha, ln0_g, ln0_b, w_in[0], hg_lb, hg_norm_g[0],
                w_hg_branch[0], pool_w[0], pool_scale[0], w_out[0], ln1_g[0], ln1_b[0])
    ids, gates, base = _router(h1, p[0].reshape(t, -1), alpha, w_query[0], sub_keys[0],
                               w_ple_gate[0], w_ple_proj[0])
    ids_flat = ids.reshape(-1)
    act = _peer_u(ids_flat, h1, gates, _pack_table(u_tab[0]))
    out = _peer_v(ids_flat, act, base, _pack_table(v_tab[0]), ln2_g[0], ln2_b[0])
    return out.reshape(batch, seq, d)
```

```python
import functools

import jax
import jax.numpy as jnp
from jax import lax
from jax.experimental import pallas as pl
from jax.experimental.pallas import tpu as pltpu

F32 = jnp.float32
BF16 = jnp.bfloat16

HG_HEADS = 8
HG_DK = 128
HG_CHUNK = 32
POOL_WINDOWS = (2, 4, 8, 16)
POOL_GROUP = 128
MAX_WIN = 16
PEER_HEADS = 8
PEER_NKEYS = 128
PEER_HALF = 128
PEER_TOPK = 16
LN_EPS = 1e-5
RMS_EPS = 1e-6

LANES = 128
SUBLANES = 8

MIXER_TOKENS = 256
ROUTER_TOKENS = 1024
PEER_TOKENS = 256
PACK_ROWS = 4
PACK_BLOCK_ROWS = 512
SC_PITCH = 136


def _layer_norm(x, g, b):
    mu = jnp.mean(x, axis=-1, keepdims=True)
    xc = x - mu
    var = jnp.mean(xc * xc, axis=-1, keepdims=True)
    return xc * lax.rsqrt(var + LN_EPS) * g + b


def _sigmoid(x):
    return 1.0 / (1.0 + jnp.exp(-x))


def _dot_nt(a, b):
    return lax.dot_general(a, b, (((1,), (1,)), ((), ())), preferred_element_type=F32)


def _dot_tn(a, b):
    return lax.dot_general(a, b, (((0,), (0,)), ((), ())), preferred_element_type=F32)


def _mixer_kernel(alpha, x_ref, ln0g_ref, ln0b_ref, win_ref, hglb_ref, ng_ref, whg_ref, pw_ref,
                  ps_ref, wout_ref, ln1g_ref, ln1b_ref, h1_ref,
                  st_ref, ext_ref, qd_ref, ki_ref, ke_ref, vv_ref, dec_ref, o_ref, yb_ref, oi_ref):
    ts, d = x_ref.shape
    nc = ts // HG_CHUNK
    width = HG_HEADS * HG_DK
    pool_width = len(POOL_WINDOWS) * POOL_GROUP
    s_idx = pl.program_id(1)

    @pl.when(s_idx == 0)
    def _():
        st_ref[...] = jnp.zeros_like(st_ref)
        ext_ref[0:MAX_WIN, :] = jnp.zeros((MAX_WIN, pool_width), F32)

    h0 = _layer_norm(x_ref[...], ln0g_ref[...], ln0b_ref[...])
    hb = h0.astype(BF16)

    def proj(lo, hi):
        return jnp.dot(hb, win_ref[:, lo:hi], preferred_element_type=F32)

    lbl = hglb_ref[...]
    lmax = jnp.max(lbl, axis=0, keepdims=True)
    lexp = jnp.exp(lbl - lmax)
    lb = lexp[0:1, :] / jnp.sum(lexp, axis=0, keepdims=True)

    f = lb + (1.0 - lb) * _sigmoid(proj(width, 2 * width))
    kk = 1.0 - f
    bh = jnp.log(f)
    row_in_chunk = lax.broadcasted_iota(jnp.int32, (ts, 1), 0) % HG_CHUNK
    sh = 1
    while sh < HG_CHUNK:
        bh = bh + jnp.where(row_in_chunk >= sh, pltpu.roll(bh, sh, axis=0), 0.0)
        sh *= 2
    bh3 = bh.reshape(nc, HG_CHUNK, width)
    bl3 = bh3[:, HG_CHUNK - 1:HG_CHUNK, :]
    dec_ref[...] = jnp.exp(bl3.reshape(nc, width))
    ke_ref[...] = (kk * jnp.exp(jnp.broadcast_to(bl3, bh3.shape).reshape(ts, width) - bh)).astype(BF16)
    ki_ref[...] = (kk * jnp.exp(-bh)).astype(BF16)
    q = proj(0, width)
    qd_ref[...] = (q * _sigmoid(q) * (HG_DK ** -0.5) * jnp.exp(bh)).astype(BF16)
    vv_ref[...] = proj(2 * width, 3 * width).astype(BF16)
    g_out = proj(3 * width, 4 * width)
    g_act = g_out * _sigmoid(g_out)

    r_i = lax.broadcasted_iota(jnp.int32, (ts, ts), 0)
    c_i = lax.broadcasted_iota(jnp.int32, (ts, ts), 1)
    causal = (r_i // HG_CHUNK == c_i // HG_CHUNK) & (c_i <= r_i)

    for c in range(nc):
        rows = slice(c * HG_CHUNK, (c + 1) * HG_CHUNK)
        for h in range(HG_HEADS):
            cols = slice(h * HG_DK, (h + 1) * HG_DK)
            st = st_ref[h]
            oi_ref[rows, cols] = _dot_nt(qd_ref[rows, cols], st.astype(BF16))
            st_ref[h] = st * dec_ref[c:c + 1, cols] + _dot_tn(vv_ref[rows, cols], ke_ref[rows, cols])

    for h in range(HG_HEADS):
        cols = slice(h * HG_DK, (h + 1) * HG_DK)
        scores = jnp.where(causal, _dot_nt(qd_ref[:, cols], ki_ref[:, cols]), 0.0)
        o_h = jnp.dot(scores.astype(BF16), vv_ref[:, cols], preferred_element_type=F32) + oi_ref[:, cols]
        o_h = o_h * lax.rsqrt(jnp.mean(o_h * o_h, axis=-1, keepdims=True) + RMS_EPS) * ng_ref[...]
        o_ref[:, cols] = (o_h * g_act[:, cols]).astype(BF16)
    y_a = jnp.dot(o_ref[...], whg_ref[...], preferred_element_type=F32)

    v_pool = proj(4 * width, 4 * width + pool_width)
    ext_ref[MAX_WIN:MAX_WIN + ts, :] = v_pool
    pos = s_idx * ts + lax.broadcasted_iota(jnp.int32, (ts, 1), 0)
    out_group = d // len(POOL_WINDOWS)
    for gi, w in enumerate(POOL_WINDOWS):
        gcols = slice(gi * POOL_GROUP, (gi + 1) * POOL_GROUP)
        acc = ext_ref[:, gcols]
        span = 1
        while span < w:
            acc = acc + pltpu.roll(acc, span, axis=0)
            span *= 2
        wsum = acc[MAX_WIN:, :]
        cnt = jnp.minimum(pos + 1, w).astype(F32)
        pooled = wsum / cnt - v_pool[:, gcols]
        yb_ref[:, gi * out_group:(gi + 1) * out_group] = jnp.dot(
            pooled.astype(BF16), pw_ref[gi], preferred_element_type=F32)
    ext_ref[0:MAX_WIN, :] = ext_ref[ts:ts + MAX_WIN, :]
    y_b = yb_ref[...] * ps_ref[...]

    gate_a = proj(4 * width + pool_width, 4 * width + pool_width + d)
    gate_b = proj(4 * width + pool_width + d, 4 * width + pool_width + 2 * d)
    mix = _sigmoid(gate_a) * y_a + _sigmoid(gate_b) * y_b
    y = alpha * h0 + jnp.dot(mix.astype(BF16), wout_ref[...], preferred_element_type=F32)
    h1_ref[...] = _layer_norm(y, ln1g_ref[...], ln1b_ref[...])


def _const_spec(shape):
    nd = len(shape)
    return pl.BlockSpec(shape, lambda *_: (0,) * nd, pipeline_mode=pl.Buffered(1))


def _mixer(x2, batch, seq, alpha, ln0_g, ln0_b, w_in, hg_lb, norm_g, w_hg, pool_w, pool_scale,
           w_out, ln1_g, ln1_b):
    t, d = x2.shape
    ts = MIXER_TOKENS
    ns = seq // ts
    width = HG_HEADS * HG_DK
    pool_width = len(POOL_WINDOWS) * POOL_GROUP
    row = lambda a: a.reshape(1, -1)
    args = (x2, row(ln0_g), row(ln0_b), w_in.astype(BF16), hg_lb, row(norm_g), w_hg.astype(BF16),
            pool_w.astype(BF16), row(pool_scale), w_out.astype(BF16), row(ln1_g), row(ln1_b))
    in_specs = [pl.BlockSpec((ts, d), lambda b, s: (b * ns + s, 0))]
    in_specs += [_const_spec(a.shape) for a in args[1:]]
    return pl.pallas_call(
        functools.partial(_mixer_kernel, alpha),
        grid=(batch, ns),
        in_specs=in_specs,
        out_specs=pl.BlockSpec((ts, d), lambda b, s: (b * ns + s, 0)),
        out_shape=jax.ShapeDtypeStruct((t, d), F32),
        scratch_shapes=[
            pltpu.VMEM((HG_HEADS, HG_DK, HG_DK), F32),
            pltpu.VMEM((ts + MAX_WIN, pool_width), F32),
            pltpu.VMEM((ts, width), BF16),
            pltpu.VMEM((ts, width), BF16),
            pltpu.VMEM((ts, width), BF16),
            pltpu.VMEM((ts, width), BF16),
            pltpu.VMEM((ts // HG_CHUNK, width), F32),
            pltpu.VMEM((ts, width), BF16),
            pltpu.VMEM((ts, d), F32),
            pltpu.VMEM((ts, width), F32),
        ],
        compiler_params=pltpu.CompilerParams(
            dimension_semantics=("arbitrary", "arbitrary"),
            vmem_limit_bytes=48 * 1024 * 1024),
        name="mixer",
    )(*args)


def _sort_network(n):
    def merge(lo, hi, r):
        step = r * 2
        if step < hi - lo:
            yield from merge(lo, hi, step)
            yield from merge(lo + r, hi, step)
            yield from [(i, i + r) for i in range(lo + r, hi - r, step)]
        else:
            yield (lo, lo + r)

    def sort(lo, hi):
        if hi - lo >= 1:
            mid = lo + (hi - lo) // 2
            yield from sort(lo, mid)
            yield from sort(mid + 1, hi)
            yield from merge(lo, hi, 1)

    return tuple(sort(0, n - 1))


def _bitonic_merge_network(n):
    pairs, dist = [], n // 2
    while dist:
        pairs += [(i, i + dist) for i in range(n) if not i & dist]
        dist //= 2
    return tuple(pairs)


def _ordered(a, b):
    a_first = (a[0] > b[0]) | ((a[0] == b[0]) & (a[1] < b[1]))
    first = (jnp.maximum(a[0], b[0]),) + tuple(jnp.where(a_first, x, y) for x, y in zip(a[1:], b[1:]))
    second = (jnp.minimum(a[0], b[0]),) + tuple(jnp.where(a_first, y, x) for x, y in zip(a[1:], b[1:]))
    return first, second


def _top_k_sorted(elems, k):
    return _merge_sorted_blocks([_apply_network(elems[s:s + k], _sort_network(k))
                                 for s in range(0, len(elems), k)])


def _apply_network(elems, network):
    elems = list(elems)
    for i, j in network:
        elems[i], elems[j] = _ordered(elems[i], elems[j])
    return elems


def _merge_sorted_blocks(blocks):
    k = len(blocks[0])
    while len(blocks) > 1:
        blocks = [_apply_network([_ordered(a[i], b[k - 1 - i])[0] for i in range(k)], _bitonic_merge_network(k))
                  for a, b in zip(blocks[0::2], blocks[1::2])]
    return blocks[0]


def _router_kernel(alpha, h_ref, p_ref, wq_ref, keys_ref, wpg_ref, wpp_ref, ids_ref, gates_ref, base_ref,
                   q_ref, sc_ref, ids_s, gates_s):
    h = h_ref[...]
    hb = h.astype(BF16)
    tb = h.shape[0]
    k = PEER_TOPK
    planes = tb // LANES
    vreg = (planes, LANES)
    ple = _sigmoid(jnp.dot(hb, wpg_ref[...], preferred_element_type=F32)) * jnp.dot(
        p_ref[...].astype(BF16), wpp_ref[...], preferred_element_type=F32)
    base_ref[...] = alpha * h + ple

    for hd in range(PEER_HEADS):
        cols = slice(hd * 2 * PEER_HALF, (hd + 1) * 2 * PEER_HALF)
        q = jnp.dot(hb, wq_ref[:, cols], preferred_element_type=F32).astype(BF16)
        q_ref[2 * hd] = q[:, :PEER_HALF]
        q_ref[2 * hd + 1] = q[:, PEER_HALF:]

    def per_key_rows(ref, row):
        return ref.at[pl.ds(row, planes, stride=LANES), :]

    def head(hd, carry):
        tops = []
        for half in range(2):
            sc = _dot_nt(keys_ref[hd, half], q_ref[2 * hd + half])
            for pln in range(planes):
                sc_ref[pln * SC_PITCH:pln * SC_PITCH + LANES, :] = sc[:, pln * LANES:(pln + 1) * LANES]
            elems = [(sc_ref[pl.ds(key, planes, stride=SC_PITCH), :], jnp.full(vreg, float(key), F32))
                     for key in range(PEER_NKEYS)]
            tops.append(_top_k_sorted(elems, k))
        (top1, top2) = tops
        def cand(a, b):
            return (top1[a][0] + top2[b][0], jnp.full(vreg, float(a * k + b), F32),
                    top1[a][1] * PEER_NKEYS + top2[b][1])

        def pad(i):
            return (jnp.full(vreg, -jnp.inf, F32), jnp.full(vreg, float(k * k + i), F32), jnp.zeros(vreg, F32))

        rows = [[cand(a, b) for b in range(k // (a + 1))] for a in range(k)]
        singles = [rows[a][0] for a in range(k // 2, k)]
        mixed = [c for a in range(2, k // 2 - 1) for c in rows[a]]
        last = rows[k // 2 - 1] + [pad(i) for i in range(k - len(rows[k // 2 - 1]))]
        assert len(rows[0]) == k and len(rows[1]) + len(singles) == k and len(mixed) == k
        best = _merge_sorted_blocks([
            rows[0],
            _apply_network(rows[1] + singles[::-1], _bitonic_merge_network(k)),
            _apply_network(mixed, _sort_network(k)),
            last])
        ex = [jnp.exp(e[0] - best[0][0]) for e in best]
        denom = functools.reduce(jnp.add, ex)
        for j in range(k):
            slot = hd * k + j
            per_key_rows(gates_s, slot)[...] = ex[j] / denom
            per_key_rows(ids_s, slot)[...] = best[j][2] * PACK_ROWS
        return carry

    lax.fori_loop(0, PEER_HEADS, head, 0)
    for pln in range(planes):
        rows = slice(pln * LANES, (pln + 1) * LANES)
        gates_ref[rows, :] = gates_s[rows, :].T
        ids_ref[rows, :] = ids_s[rows, :].T.astype(jnp.int32)


def _router(h1, p2, alpha, w_query, sub_keys, w_ple_gate, w_ple_proj):
    t, d = h1.shape
    tb = ROUTER_TOKENS
    slots = PEER_HEADS * PEER_TOPK
    assert slots == LANES and PEER_NKEYS == LANES
    args = (h1, p2, w_query.astype(BF16), sub_keys.astype(BF16), w_ple_gate.astype(BF16),
            w_ple_proj.astype(BF16))
    in_specs = [pl.BlockSpec((tb, d), lambda i: (i, 0)),
                pl.BlockSpec((tb, p2.shape[1]), lambda i: (i, 0))]
    in_specs += [_const_spec(a.shape) for a in args[2:]]
    return pl.pallas_call(
        functools.partial(_router_kernel, alpha),
        grid=(t // tb,),
        in_specs=in_specs,
        out_specs=[pl.BlockSpec((tb, slots), lambda i: (i, 0)),
                   pl.BlockSpec((tb, slots), lambda i: (i, 0)),
                   pl.BlockSpec((tb, d), lambda i: (i, 0))],
        out_shape=[jax.ShapeDtypeStruct((t, slots), jnp.int32),
                   jax.ShapeDtypeStruct((t, slots), F32),
                   jax.ShapeDtypeStruct((t, d), F32)],
        scratch_shapes=[pltpu.VMEM((2 * PEER_HEADS, tb, PEER_HALF), BF16),
                        pltpu.VMEM((tb // LANES * SC_PITCH, LANES), F32),
                        pltpu.VMEM((tb, LANES), F32),
                        pltpu.VMEM((tb, LANES), F32)],
        compiler_params=pltpu.CompilerParams(
            dimension_semantics=("arbitrary",),
            vmem_limit_bytes=56 * 1024 * 1024),
        name="router",
    )(*args)


def _pack_kernel(tab_ref, out_ref):
    half = tab_ref.shape[1] // 2
    words = pltpu.bitcast(pltpu.pack_elementwise([tab_ref[:, :half], tab_ref[:, half:]], packed_dtype=BF16),
                          jnp.uint32)
    rows = tab_ref.shape[0]
    for j in range(PACK_ROWS):
        out_ref[pl.ds(j, rows, stride=PACK_ROWS), :] = words[:, j * LANES:(j + 1) * LANES]


def _pack_table(tab):
    n, d = tab.shape
    rows = PACK_BLOCK_ROWS
    return pl.pallas_call(
        _pack_kernel,
        grid=(n // rows,),
        in_specs=[pl.BlockSpec((rows, d), lambda i: (i, 0))],
        out_specs=pl.BlockSpec((rows * PACK_ROWS, LANES), lambda i: (i, 0)),
        out_shape=jax.ShapeDtypeStruct((n * PACK_ROWS, LANES), jnp.uint32),
        name="pack",
    )(tab)


def _gather_rows(ids_ref, tab_ref, g_ref, t, slots):
    ids_t = ids_ref.at[pl.ds(t * slots, slots)]
    for k in range(slots):
        slab = tab_ref[pl.ds(pl.multiple_of(ids_t[k], PACK_ROWS), PACK_ROWS), :]
        start = (k // SUBLANES) * (PACK_ROWS * SUBLANES) + k % SUBLANES
        g_ref[pl.ds(start, PACK_ROWS, stride=SUBLANES), :] = slab


def _word_tile(g_ref, slots, c):
    return jnp.concatenate(
        [g_ref[(kt * PACK_ROWS + c) * SUBLANES:(kt * PACK_ROWS + c + 1) * SUBLANES, :]
         for kt in range(slots // SUBLANES)], axis=0)


def _gathered_matrix(g_ref, slots):
    words = jnp.concatenate([_word_tile(g_ref, slots, c) for c in range(PACK_ROWS)], axis=1)
    return pltpu.bitcast(words, BF16)


def _pipelined_tokens(tb, gather, compute, bufs):
    g0, g1 = bufs
    gather(0, g0)
    gather(1, g1)

    def pair(i, carry):
        t = 2 * i
        compute(t, g0)
        gather(t + 2, g0)
        compute(t + 1, g1)
        gather(t + 3, g1)
        return carry

    lax.fori_loop(0, tb // 2 - 1, pair, 0)
    compute(tb - 2, g0)
    compute(tb - 1, g1)


def _pair_select(slots):
    r = lax.broadcasted_iota(jnp.int32, (2 * slots, slots), 0)
    c = lax.broadcasted_iota(jnp.int32, (2 * slots, slots), 1)
    return (r // 2 == c).astype(F32)


def _gelu(x):
    return 0.5 * x * (1.0 + lax.erf(x * (2.0 ** -0.5)))


def _split_bf16(x):
    head = x.astype(BF16).astype(F32)
    return head, x - head


def _store_rows(ref, j, tb, value):
    for c in range(ref.shape[0]):
        ref[c, pl.ds(j, tb, stride=SUBLANES), :] = value[:, c * LANES:(c + 1) * LANES]


def _load_rows(ref, j, tb):
    return jnp.concatenate([ref[c, pl.ds(j, tb, stride=SUBLANES), :] for c in range(ref.shape[0])], axis=1)


def _load_group(ref, rows):
    return jnp.concatenate([ref[c, rows, :] for c in range(ref.shape[0])], axis=1)


def _store_group(ref, rows, value):
    for c in range(ref.shape[0]):
        ref[c, rows, :] = value[:, c * LANES:(c + 1) * LANES]


def _peer_u_kernel(ids_ref, h_ref, gate_ref, tab_ref, act_ref, g0_ref, g1_ref, q0_ref, q1_ref, lhs_ref, xq_ref,
                   ra_ref, rq_ref):
    tb, d = h_ref.shape
    slots = gate_ref.shape[1]
    half = d // 2
    parts = 4
    mx = slots // 2

    @pl.when(pl.program_id(0) == 0)
    def _():
        lhs_ref[...] = jnp.zeros_like(lhs_ref)

    x = h_ref[...]
    x_hi, x_lo = _split_bf16(x)
    for j, part in enumerate((x_hi[:, :half], x_lo[:, :half], x_hi[:, half:], x_lo[:, half:])):
        _store_rows(lhs_ref, j, tb, part)
    for r in range(SUBLANES):
        c = (r // 2) + (r % 2) * PACK_ROWS
        xq_ref[pl.ds(r, tb, stride=SUBLANES), :] = x[:, c * LANES:(c + 1) * LANES]

    def gather(t, bufs):
        g_ref, q_ref = bufs
        ids_t = ids_ref.at[pl.ds(t * slots, slots)]
        xq = xq_ref[pl.ds(pl.multiple_of(t * SUBLANES, SUBLANES), SUBLANES), :]
        for k in range(slots):
            slab = tab_ref[pl.ds(pl.multiple_of(ids_t[k], PACK_ROWS), PACK_ROWS), :]
            if k < mx:
                start = (k // SUBLANES) * (PACK_ROWS * SUBLANES) + k % SUBLANES
                g_ref[pl.ds(start, PACK_ROWS, stride=SUBLANES), :] = slab
            else:
                prod = pltpu.bitcast(slab, BF16).astype(F32) * xq
                q_ref[k - mx:k - mx + 1, :] = jnp.sum(prod, axis=0, keepdims=True)

    ones = jnp.ones((SUBLANES, LANES), BF16)

    def compute(t, bufs):
        g_ref, q_ref = bufs
        rows = pl.ds(pl.multiple_of(t * SUBLANES, SUBLANES), SUBLANES)
        ra_ref[rows, :] = _dot_nt(_load_group(lhs_ref, rows).astype(BF16), _gathered_matrix(g_ref, mx))
        q_hi, q_lo = _split_bf16(q_ref[...])
        rq_ref[rows, 0:mx] = _dot_nt(ones, q_hi.astype(BF16)) + _dot_nt(ones, q_lo.astype(BF16))

    _pipelined_tokens(tb, gather, compute, ((g0_ref, q0_ref), (g1_ref, q1_ref)))

    r = [ra_ref[pl.ds(j, tb, stride=SUBLANES), :] for j in range(parts)]
    lane_even = lax.broadcasted_iota(jnp.int32, (1, 2 * mx), 1) % 2 == 0
    inter = jnp.where(lane_even, r[0] + r[1], r[2] + r[3])
    pre_a = jnp.dot(inter, _pair_select(mx), preferred_element_type=F32, precision=lax.Precision.HIGHEST)
    pre_b = rq_ref[pl.ds(0, tb, stride=SUBLANES), :][:, 0:mx]
    act_ref[...] = _gelu(jnp.concatenate([pre_a, pre_b], axis=1)) * gate_ref[...]


def _peer_v_kernel(ids_ref, act_ref, base_ref, tab_ref, lng_ref, lnb_ref, out_ref, g0_ref, g1_ref, lhs_ref,
                   r_ref):
    tb, d = base_ref.shape
    slots = act_ref.shape[1]

    @pl.when(pl.program_id(0) == 0)
    def _():
        lhs_ref[...] = jnp.zeros_like(lhs_ref)

    a2 = lax.dot_general(act_ref[...], _pair_select(slots), (((1,), (1,)), ((), ())),
                         preferred_element_type=F32, precision=lax.Precision.HIGHEST)
    lane_even = lax.broadcasted_iota(jnp.int32, (1, 2 * slots), 1) % 2 == 0
    a_hi, a_lo = _split_bf16(a2)
    for j, part in enumerate((jnp.where(lane_even, a_hi, 0.0), jnp.where(lane_even, a_lo, 0.0),
                              jnp.where(lane_even, 0.0, a_hi), jnp.where(lane_even, 0.0, a_lo))):
        _store_rows(lhs_ref, j, tb, part)

    def gather(t, g_ref):
        _gather_rows(ids_ref, tab_ref, g_ref, t, slots)

    def compute(t, g_ref):
        rows = pl.ds(pl.multiple_of(t * SUBLANES, SUBLANES), SUBLANES)
        _store_group(r_ref, rows, jnp.dot(_load_group(lhs_ref, rows).astype(BF16),
                                          _gathered_matrix(g_ref, slots), preferred_element_type=F32))

    _pipelined_tokens(tb, gather, compute, (g0_ref, g1_ref))

    r = [_load_rows(r_ref, j, tb) for j in range(4)]
    ffn = jnp.concatenate([r[0] + r[1], r[2] + r[3]], axis=1)
    out_ref[...] = _layer_norm(base_ref[...] + ffn, lng_ref[...], lnb_ref[...])


def _peer_specs(tb, d, slots, tab):
    ids_spec = pl.BlockSpec((tb * slots,), lambda i: (i,), memory_space=pltpu.SMEM)
    tile_spec = pl.BlockSpec((tb, d), lambda i: (i, 0))
    slot_spec = pl.BlockSpec((tb, slots), lambda i: (i, 0))
    return ids_spec, tile_spec, slot_spec, _const_spec(tab.shape)


def _peer_u(ids_flat, h1, gates, tab):
    t, d = h1.shape
    slots = gates.shape[1]
    tb = PEER_TOKENS
    ids_spec, tile_spec, slot_spec, tab_spec = _peer_specs(tb, d, slots, tab)
    return pl.pallas_call(
        _peer_u_kernel,
        grid=(t // tb,),
        in_specs=[ids_spec, tile_spec, slot_spec, tab_spec],
        out_specs=slot_spec,
        out_shape=jax.ShapeDtypeStruct((t, slots), F32),
        scratch_shapes=[
            *[pltpu.VMEM((slots // 2 * PACK_ROWS, LANES), jnp.uint32)] * 2,
            *[pltpu.VMEM((slots // 2, LANES), F32)] * 2,
            pltpu.VMEM((d // 2 // LANES, tb * SUBLANES, LANES), F32),
            pltpu.VMEM((tb * SUBLANES, LANES), F32),
            pltpu.VMEM((tb * SUBLANES, slots), F32),
            pltpu.VMEM((tb * SUBLANES, LANES), F32),
        ],
        compiler_params=pltpu.CompilerParams(
            dimension_semantics=("arbitrary",),
            vmem_limit_bytes=52 * 1024 * 1024),
        name="peer_u",
    )(ids_flat, h1, gates, tab)


def _peer_v(ids_flat, act, base, tab, ln_g, ln_b):
    t, d = base.shape
    slots = act.shape[1]
    tb = PEER_TOKENS
    ids_spec, tile_spec, slot_spec, tab_spec = _peer_specs(tb, d, slots, tab)
    row = lambda a: a.reshape(1, -1)
    return pl.pallas_call(
        _peer_v_kernel,
        grid=(t // tb,),
        in_specs=[ids_spec, slot_spec, tile_spec, tab_spec, _const_spec((1, d)), _const_spec((1, d))],
        out_specs=tile_spec,
        out_shape=jax.ShapeDtypeStruct((t, d), F32),
        scratch_shapes=[
            *[pltpu.VMEM((slots * PACK_ROWS, LANES), jnp.uint32)] * 2,
            pltpu.VMEM((2 * slots // LANES, tb * SUBLANES, LANES), F32),
            pltpu.VMEM((d // 2 // LANES, tb * SUBLANES, LANES), F32),
        ],
        compiler_params=pltpu.CompilerParams(
            dimension_semantics=("arbitrary",),
            vmem_limit_bytes=52 * 1024 * 1024),
        name="peer_v",
    )(ids_flat, act, base, tab, row(ln_g), row(ln_b))


def kernel(x, p, ln0_g, ln0_b, w_in, hg_lb, hg_norm_g, w_hg_branch, pool_w, pool_scale, w_out,
           ln1_g, ln1_b, w_query, sub_keys, u_tab, v_tab, w_ple_gate, w_ple_proj, ln2_g, ln2_b):
    batch, seq, d = x.shape
    depth = w_in.shape[0]
    t = batch * seq
    assert depth == 1 and seq % MIXER_TOKENS == 0 and t % ROUTER_TOKENS == 0 and t % PEER_TOKENS == 0
    alpha = (2.0 * depth) ** 0.25
    h1 = _mixer(x.reshape(t, d), batch, seq, alpha, ln0_g, ln0_b, w_in[0], hg_lb, hg_norm_g[0],
                w_hg_branch[0], pool_w[0], pool_scale[0], w_out[0], ln1_g[0], ln1_b[0])
    ids, gates, base = _router(h1, p[0].reshape(t, -1), alpha, w_query[0], sub_keys[0],
                               w_ple_gate[0], w_ple_proj[0])
    ids_flat = ids.reshape(-1)
    act = _peer_u(ids_flat, h1, gates, _pack_table(u_tab[0]))
    out = _peer_v(ids_flat, act, base, _pack_table(v_tab[0]), ln2_g[0], ln2_b[0])
    return out.reshape(batch, seq, d)
```
